```python
import math
import jax, jax.numpy as jnp
from jax import lax
import numpy as np

D_MODEL = 1024
BATCH = 16
SEQ = 2048
DEPTH = 4

N_Q_HEADS = 8
N_KV_HEADS = 2
HEAD_DIM = 64
Q_PER_KV = N_Q_HEADS // N_KV_HEADS
WINDOW = 128
BLOCK = WINDOW
ATTN_WIDTH = N_Q_HEADS * HEAD_DIM
KV_WIDTH = N_KV_HEADS * HEAD_DIM
N_BUCKETS = 32
MAX_DISTANCE = 128
N_SG_GROUPS = 8
SG_GROUP_DIM = 64
SG_CHUNK = 128
SG_WIDTH = N_SG_GROUPS * SG_GROUP_DIM
IN_SPLITS = (ATTN_WIDTH,
             ATTN_WIDTH + KV_WIDTH,
             ATTN_WIDTH + 2 * KV_WIDTH,
             ATTN_WIDTH + 2 * KV_WIDTH + SG_WIDTH,
             ATTN_WIDTH + 2 * KV_WIDTH + 2 * SG_WIDTH,
             ATTN_WIDTH + 2 * KV_WIDTH + 2 * SG_WIDTH + D_MODEL)
IN_WIDTH = ATTN_WIDTH + 2 * KV_WIDTH + 2 * SG_WIDTH + 2 * D_MODEL
N_EXPERTS = 16
N_EXPERT_GROUPS = 4
EXPERTS_PER_GROUP = N_EXPERTS // N_EXPERT_GROUPS
TOP_K = 2
D_FF_EXPERT = 256
DEEPNORM_ALPHA = (2 * DEPTH) ** 0.25
DEEPNORM_BETA = (8 * DEPTH) ** -0.25
ADA_SCALE = 0.2
LN_EPS = 1e-5
NEG_INF = -1e30

kernel_name = 'hybrid_swa_sink_gmlp_groupmoe_deepnorm_adaln'


def layer_norm(x, gain=None, bias=None):
    xf = x.astype(jnp.float32)
    mu = jnp.mean(xf, axis=-1, keepdims=True)
    var = jnp.mean(jnp.square(xf - mu), axis=-1, keepdims=True)
    y = (xf - mu) * lax.rsqrt(var + LN_EPS)
    if gain is not None:
        y = y * gain.astype(jnp.float32) + bias.astype(jnp.float32)
    return y.astype(x.dtype)


def t5_causal_bucket(rel):
    n = jnp.maximum(rel, 0)
    max_exact = N_BUCKETS // 2
    nf = jnp.maximum(n, 1).astype(jnp.float32)
    large = max_exact + (jnp.log(nf / max_exact) / math.log(MAX_DISTANCE / max_exact)
                         * (N_BUCKETS - max_exact)).astype(jnp.int32)
    large = jnp.minimum(large, N_BUCKETS - 1)
    return jnp.where(n < max_exact, n, large)


def relative_band_bias(rel_bias_table):
    qi = jnp.arange(BLOCK)[:, None]
    kj = jnp.arange(2 * BLOCK)[None, :]
    rel = qi + BLOCK - kj
    in_window = (rel >= 0) & (rel < WINDOW)
    bias = rel_bias_table[t5_causal_bucket(rel)]
    bias = jnp.transpose(bias, (2, 0, 1)).reshape(N_KV_HEADS, Q_PER_KV, BLOCK, 2 * BLOCK)
    return bias.astype(jnp.float32), in_window


def sliding_window_attention(q, k, v, sinks, band_bias, in_window):
    B, S = q.shape[0], q.shape[1]
    nb = S // BLOCK
    qb = q.reshape(B, nb, BLOCK, N_KV_HEADS, Q_PER_KV, HEAD_DIM)

    def band(t):
        prev = jnp.pad(t, ((0, 0), (BLOCK, 0), (0, 0)))[:, :S]
        both = jnp.concatenate([prev.reshape(B, nb, BLOCK, KV_WIDTH),
                                t.reshape(B, nb, BLOCK, KV_WIDTH)], axis=2)
        return both.reshape(B, nb, 2 * BLOCK, N_KV_HEADS, HEAD_DIM)

    kb, vb = band(k), band(v)
    scores = jnp.einsum('bnqhgd,bnkhd->bnhgqk', qb, kb).astype(jnp.float32) * (HEAD_DIM ** -0.5)
    key_pos = jnp.arange(nb)[:, None] * BLOCK - BLOCK + jnp.arange(2 * BLOCK)[None, :]
    mask = in_window[None] & (key_pos >= 0)[:, None, :]
    scores = jnp.where(mask[None, :, None, None], scores + band_bias[None, None], NEG_INF)
    sink = jnp.broadcast_to(sinks.astype(jnp.float32).reshape(1, 1, N_KV_HEADS, Q_PER_KV, 1, 1),
                            scores.shape[:-1] + (1,))
    probs = jax.nn.softmax(jnp.concatenate([scores, sink], axis=-1), axis=-1)[..., :-1]
    out = jnp.einsum('bnhgqk,bnkhd->bnqhgd', probs.astype(vb.dtype), vb)
    return out.reshape(B, S, ATTN_WIDTH)


def chunked_spatial_gating(u, v, sg_w, sg_b, ln_g, ln_b):
    v = layer_norm(v, ln_g, ln_b)
    B, S = v.shape[0], v.shape[1]
    nc = S // SG_CHUNK
    vc = v.reshape(B, nc, SG_CHUNK, N_SG_GROUPS, SG_GROUP_DIM)
    causal = jnp.tril(jnp.ones((SG_CHUNK, SG_CHUNK), dtype=bool))
    w = jnp.where(causal[None], sg_w, 0.0).astype(v.dtype)
    mixed = jnp.einsum('gij,bcjgd->bcigd', w, vc) + sg_b.T.astype(v.dtype)[None, None, :, :, None]
    return u * mixed.reshape(B, S, SG_WIDTH)


def token_mixing(h, w_in, w_br_attn, w_br_sg, w_out, sinks, band_bias, in_window,
                 sg_w, sg_b, sg_ln_g, sg_ln_b):
    proj = h @ w_in
    q, k, v, su, sv, ga, gs = jnp.split(proj, IN_SPLITS, axis=-1)
    attn = sliding_window_attention(q, k, v, sinks, band_bias, in_window)
    sg = chunked_spatial_gating(jax.nn.gelu(su, approximate=False), jax.nn.gelu(sv, approximate=False),
                                sg_w, sg_b, sg_ln_g, sg_ln_b)
    merged = jax.nn.sigmoid(ga) * (attn @ w_br_attn) + jax.nn.sigmoid(gs) * (sg @ w_br_sg)
    return merged @ w_out


def grouped_moe(h, w_router, router_bias, w_gate, w_up, w_down):
    B, S, D = h.shape
    t = h.reshape(B * S, D)
    T = t.shape[0]
    probs = jax.nn.softmax((t @ w_router).astype(jnp.float32), axis=-1)
    sel = (probs + router_bias.astype(jnp.float32)).reshape(T, N_EXPERT_GROUPS, EXPERTS_PER_GROUP)
    group_score = jnp.sum(lax.top_k(sel, TOP_K)[0], axis=-1)
    best = jnp.argmax(group_score, axis=-1)
    idx = jnp.broadcast_to(best[:, None, None], (T, 1, EXPERTS_PER_GROUP))
    in_group = jnp.take_along_axis(sel, idx, axis=1)[:, 0]
    _, local = lax.top_k(in_group, TOP_K)
    expert_idx = best[:, None] * EXPERTS_PER_GROUP + local
    w_sel = jnp.take_along_axis(probs, expert_idx, axis=-1)
    w_sel = w_sel / jnp.sum(w_sel, axis=-1, keepdims=True)
    combine = jnp.sum(jax.nn.one_hot(expert_idx, N_EXPERTS, dtype=jnp.float32) * w_sel[..., None], axis=1)
    combine = combine.astype(t.dtype)
    out = jnp.zeros_like(t)
    for e in range(N_EXPERTS):
        hid = jax.nn.silu(t @ w_gate[e]) * (t @ w_up[e])
        out = out + combine[:, e:e + 1] * (hid @ w_down[e])
    return out.reshape(B, S, D)


def setup_inputs(seed: int = 0) -> dict:
    key = jax.random.key(seed)
    ks = jax.random.split(key, 24)
    f32 = jnp.float32
    L, D, E, F = DEPTH, D_MODEL, N_EXPERTS, D_FF_EXPERT

    def nrm(k, shape, s):
        return jax.random.normal(k, shape, f32) * s

    return {
        'x': nrm(ks[0], (BATCH, SEQ, D), 1.0),
        'c': nrm(ks[1], (BATCH, D), 1.0),
        'rel_bias_table': nrm(ks[2], (N_BUCKETS, N_Q_HEADS), 0.5),
        'w_router': nrm(ks[3], (D, E), D ** -0.5),
        'router_bias': nrm(ks[4], (E,), 0.01),
        'w_ada': nrm(ks[5], (L, D, 6 * D), ADA_SCALE * D ** -0.5),
        'b_ada': nrm(ks[6], (L, 6 * D), 0.02),
        'w_in': nrm(ks[7], (L, D, IN_WIDTH), D ** -0.5),
        'sinks': nrm(ks[8], (L, N_Q_HEADS), 0.5),
        'sg_w': nrm(ks[9], (L, N_SG_GROUPS, SG_CHUNK, SG_CHUNK), SG_CHUNK ** -0.5),
        'sg_b': 1.0 + nrm(ks[10], (L, N_SG_GROUPS, SG_CHUNK), 0.1),
        'sg_ln_g': 1.0 + nrm(ks[11], (L, SG_WIDTH), 0.05),
        'sg_ln_b': nrm(ks[12], (L, SG_WIDTH), 0.02),
        'w_br_attn': nrm(ks[13], (L, ATTN_WIDTH, D), DEEPNORM_BETA * ATTN_WIDTH ** -0.5),
        'w_br_sg': nrm(ks[14], (L, SG_WIDTH, D), DEEPNORM_BETA * SG_WIDTH ** -0.5),
        'w_out': nrm(ks[15], (L, D, D), DEEPNORM_BETA * D ** -0.5),
        'ln1_g': 1.0 + nrm(ks[16], (L, D), 0.05),
        'ln1_b': nrm(ks[17], (L, D), 0.02),
        'w_gate': nrm(ks[18], (L, E, D, F), D ** -0.5),
        'w_up': nrm(ks[19], (L, E, D, F), D ** -0.5),
        'w_down': nrm(ks[20], (L, E, F, D), DEEPNORM_BETA * F ** -0.5),
        'ln2_g': 1.0 + nrm(ks[21], (L, D), 0.05),
        'ln2_b': nrm(ks[22], (L, D), 0.02),
    }


def reference(x, c, rel_bias_table, w_router, router_bias, w_ada, b_ada, w_in, sinks,
              sg_w, sg_b, sg_ln_g, sg_ln_b, w_br_attn, w_br_sg, w_out, ln1_g, ln1_b,
              w_gate, w_up, w_down, ln2_g, ln2_b):
    band_bias, in_window = relative_band_bias(rel_bias_table)
    cond = jax.nn.silu(c)
    for l in range(DEPTH):
        mod = cond @ w_ada[l] + b_ada[l]
        sh1, sc1, g1, sh2, sc2, g2 = [m[:, None, :] for m in jnp.split(mod, 6, axis=-1)]
        h = layer_norm(x) * (1.0 + sc1) + sh1
        y = token_mixing(h, w_in[l], w_br_attn[l], w_br_sg[l], w_out[l], sinks[l],
                         band_bias, in_window, sg_w[l], sg_b[l], sg_ln_g[l], sg_ln_b[l])
        x = layer_norm(DEEPNORM_ALPHA * x + (1.0 + g1) * y, ln1_g[l], ln1_b[l])
        h = layer_norm(x) * (1.0 + sc2) + sh2
        y = grouped_moe(h, w_router, router_bias, w_gate[l], w_up[l], w_down[l])
        x = layer_norm(DEEPNORM_ALPHA * x + (1.0 + g2) * y, ln2_g[l], ln2_b[l])
    return x
```

```python
import functools
import math

import numpy as np
import jax
import jax.numpy as jnp
from jax import lax
from jax.experimental import pallas as pl
from jax.experimental.pallas import tpu as pltpu

D_MODEL = 1024
DEPTH = 4
N_Q_HEADS = 8
N_KV_HEADS = 2
HEAD_DIM = 64
Q_PER_KV = N_Q_HEADS // N_KV_HEADS
WINDOW = 128
BLOCK = WINDOW
ATTN_WIDTH = N_Q_HEADS * HEAD_DIM
KV_WIDTH = N_KV_HEADS * HEAD_DIM
N_BUCKETS = 32
MAX_DISTANCE = 128
N_SG_GROUPS = 8
SG_GROUP_DIM = 64
SG_CHUNK = 128
SG_WIDTH = N_SG_GROUPS * SG_GROUP_DIM
IN_WIDTH = ATTN_WIDTH + 2 * KV_WIDTH + 2 * SG_WIDTH + 2 * D_MODEL
N_EXPERTS = 16
N_EXPERT_GROUPS = 4
EXPERTS_PER_GROUP = N_EXPERTS // N_EXPERT_GROUPS
TOP_K = 2
D_FF_EXPERT = 256
DEEPNORM_ALPHA = (2 * DEPTH) ** 0.25
LN_EPS = 1e-5
NEG_INF = -1e30

OFF_Q = 0
OFF_KV = ATTN_WIDTH
OFF_SU = ATTN_WIDTH + 2 * KV_WIDTH
OFF_SV = OFF_SU + SG_WIDTH
OFF_GA = OFF_SV + SG_WIDTH
OFF_GS = OFF_GA + D_MODEL

LANES = 128
TOKEN_TILE = 512
BLOCKS_PER_TILE = TOKEN_TILE // BLOCK
VMEM_LIMIT_BYTES = 56 * 1024 * 1024

BF16 = jnp.bfloat16
F32 = jnp.float32


def _layer_norm(x):
    mu = jnp.mean(x, axis=-1, keepdims=True)
    xc = x - mu
    var = jnp.mean(xc * xc, axis=-1, keepdims=True)
    return xc * lax.rsqrt(var + LN_EPS)


def _dot(a, b):
    return jnp.dot(a, b, preferred_element_type=F32)


def _dot_nt(a, b):
    return lax.dot_general(a, b, (((1,), (1,)), ((), ())), preferred_element_type=F32)


def _gelu(x):
    return 0.5 * x * (1.0 + lax.erf(x * (1.0 / math.sqrt(2.0))))


def _sigmoid(x):
    return 1.0 / (1.0 + jnp.exp(-x))


def _silu(x):
    return x * _sigmoid(x)


ADA_TILE = 1536


def _ada_kernel(c_ref, w_ref, b_ref, o_ref):
    cond = _silu(c_ref[...]).astype(BF16)
    o_ref[...] = _dot(cond, w_ref[...].astype(BF16)) + b_ref[...]


def _ada_modulation(c, w_ada, b_ada):
    L, D, N = w_ada.shape
    B = c.shape[0]
    return pl.pallas_call(
        _ada_kernel,
        grid=(L, N // ADA_TILE),
        in_specs=[
            pl.BlockSpec((B, D), lambda l, n: (0, 0)),
            pl.BlockSpec((None, D, ADA_TILE), lambda l, n: (l, 0, n)),
            pl.BlockSpec((None, 1, ADA_TILE), lambda l, n: (l, 0, n)),
        ],
        out_specs=pl.BlockSpec((None, B, ADA_TILE), lambda l, n: (l, 0, n)),
        out_shape=jax.ShapeDtypeStruct((L, B, N), F32),
        compiler_params=pltpu.CompilerParams(
            dimension_semantics=("arbitrary", "arbitrary"),
            vmem_limit_bytes=VMEM_LIMIT_BYTES),
        name="ada_modulation",
    )(c, w_ada, b_ada.reshape(L, 1, N))


def _band_buckets():
    qi = np.arange(BLOCK)[:, None]
    kj = np.arange(2 * BLOCK)[None, :]
    rel = qi + BLOCK - kj
    n = np.maximum(rel, 0)
    max_exact = N_BUCKETS // 2
    nf = np.maximum(n, 1).astype(np.float32)
    large = max_exact + (np.log(nf / np.float32(max_exact)) / np.float32(math.log(MAX_DISTANCE / max_exact))
                         * np.float32(N_BUCKETS - max_exact)).astype(np.int32)
    large = np.minimum(large, N_BUCKETS - 1)
    bucket = np.where(n < max_exact, n, large).astype(np.int32)
    in_window = ((rel >= 0) & (rel < WINDOW)).astype(np.int32)
    return bucket, in_window


def _bias_kernel(table_ref, bucket_ref, window_ref, o_ref):
    bucket = bucket_ref[...]
    in_window = window_ref[...] > 0
    for head in range(N_Q_HEADS):
        acc = jnp.zeros((BLOCK, 2 * BLOCK), F32)
        for b in range(N_BUCKETS):
            acc = jnp.where(bucket == b, table_ref[b, head], acc)
        o_ref[head] = jnp.where(in_window, acc, NEG_INF)


def _band_bias(rel_bias_table):
    bucket, in_window = _band_buckets()
    return pl.pallas_call(
        _bias_kernel,
        in_specs=[
            pl.BlockSpec(memory_space=pltpu.SMEM),
            pl.BlockSpec(memory_space=pltpu.VMEM),
            pl.BlockSpec(memory_space=pltpu.VMEM),
        ],
        out_specs=pl.BlockSpec(memory_space=pltpu.VMEM),
        out_shape=jax.ShapeDtypeStruct((N_Q_HEADS, BLOCK, 2 * BLOCK), F32),
        name="band_bias",
    )(rel_bias_table, jnp.asarray(bucket), jnp.asarray(in_window))


def _token_mix_kernel(x_ref, mod_ref, w_in_ref, w_bra_ref, w_brs_ref, w_out_ref, bias_ref, sink_ref,
                      sgw_ref, sgb_ref, sgln_ref, ln1_ref, o_ref, kprev_ref, vprev_ref):
    j = pl.program_id(1)

    @pl.when(j == 0)
    def _():
        kprev_ref[...] = jnp.zeros_like(kprev_ref)
        vprev_ref[...] = jnp.zeros_like(vprev_ref)

    x = x_ref[...]
    sh1 = mod_ref[0:1, :]
    sc1 = mod_ref[1:2, :]
    g1 = mod_ref[2:3, :]
    h = (_layer_norm(x) * (1.0 + sc1) + sh1).astype(BF16)

    lane = lax.broadcasted_iota(jnp.int32, (1, LANES), 1)
    lo = lane < HEAD_DIM

    q = (_dot(h, w_in_ref[:, OFF_Q:OFF_Q + ATTN_WIDTH]) * (HEAD_DIM ** -0.5)).astype(BF16)
    kv = _dot(h, w_in_ref[:, OFF_KV:OFF_KV + 2 * KV_WIDTH]).astype(BF16)
    k_full = jnp.concatenate([kprev_ref[...], kv[:, :KV_WIDTH]], axis=0)
    v_full = jnp.concatenate([vprev_ref[...], kv[:, KV_WIDTH:]], axis=0)
    kprev_ref[...] = kv[TOKEN_TILE - BLOCK:, :KV_WIDTH]
    vprev_ref[...] = kv[TOKEN_TILE - BLOCK:, KV_WIDTH:]

    k_sw = pltpu.roll(k_full, HEAD_DIM, 1)
    k_dup = (jnp.where(lo, k_full, k_sw), jnp.where(lo, k_sw, k_full))
    one = jnp.ones((), BF16)
    v_ext = (jnp.where(lo, v_full, one), jnp.where(lo, one, v_full))

    key_in_prev = lax.broadcasted_iota(jnp.int32, (1, 2 * BLOCK), 1) < BLOCK
    zero = jnp.zeros((), BF16)
    attn_blocks = []
    for n in range(BLOCKS_PER_TILE):
        r0 = n * BLOCK
        slabs = []
        for hk in range(N_KV_HEADS):
            qa = q[r0:r0 + BLOCK, hk * 256:hk * 256 + LANES]
            qb = q[r0:r0 + BLOCK, hk * 256 + LANES:hk * 256 + 2 * LANES]
            qs = jnp.concatenate([jnp.where(lo, qa, zero), jnp.where(lo, zero, qa),
                                  jnp.where(lo, qb, zero), jnp.where(lo, zero, qb)], axis=0)
            s = _dot_nt(qs, k_dup[hk][r0:r0 + 2 * BLOCK]) + bias_ref[hk]
            if n == 0:
                s = jnp.where(jnp.logical_and(j == 0, key_in_prev), NEG_INF, s)
            sink = sink_ref[hk]
            m = jnp.maximum(jnp.max(s, axis=-1, keepdims=True), sink)
            p = jnp.exp(s - m).astype(BF16)
            pv = _dot(p, v_ext[hk][r0:r0 + 2 * BLOCK])
            pv_sw = pltpu.roll(pv, HEAD_DIM, 1)
            row_sum = jnp.where(lo, pv_sw, pv) if hk == 0 else jnp.where(lo, pv, pv_sw)
            out = pv / (row_sum + jnp.exp(sink - m))
            out_sw = pltpu.roll(out, HEAD_DIM, 1)
            for pair in range(2):
                first = slice(2 * pair * BLOCK, (2 * pair + 1) * BLOCK)
                second = slice((2 * pair + 1) * BLOCK, (2 * pair + 2) * BLOCK)
                if hk == 0:
                    slabs.append(jnp.where(lo, out[first], out_sw[second]))
                else:
                    slabs.append(jnp.where(lo, out_sw[first], out[second]))
        attn_blocks.append(jnp.concatenate(slabs, axis=1).astype(BF16))
    attn = jnp.concatenate(attn_blocks, axis=0)
    a_br = _dot(attn, w_bra_ref[...])

    su = _dot(h, w_in_ref[:, OFF_SU:OFF_SU + SG_WIDTH])
    sv = _dot(h, w_in_ref[:, OFF_SV:OFF_SV + SG_WIDTH])
    u = _gelu(su)
    vn = (_layer_norm(_gelu(sv)) * sgln_ref[0:1, :] + sgln_ref[1:2, :]).astype(BF16)
    row = lax.broadcasted_iota(jnp.int32, (SG_CHUNK, 2 * SG_CHUNK), 0)
    col = lax.broadcasted_iota(jnp.int32, (SG_CHUNK, 2 * SG_CHUNK), 1)
    causal = (col & (SG_CHUNK - 1)) <= row
    mixed_cols = []
    for s_idx in range(SG_WIDTH // LANES):
        w_pair = jnp.where(causal, sgw_ref[s_idx], zero)
        b_slab = sgb_ref[:, s_idx * LANES:(s_idx + 1) * LANES]
        rows = []
        for c in range(TOKEN_TILE // SG_CHUNK):
            slab = vn[c * SG_CHUNK:(c + 1) * SG_CHUNK, s_idx * LANES:(s_idx + 1) * LANES]
            rhs = jnp.concatenate([jnp.where(lo, slab, zero), jnp.where(lo, zero, slab)], axis=0)
            rows.append(_dot(w_pair, rhs) + b_slab)
        mixed_cols.append(jnp.concatenate(rows, axis=0))
    mixed = jnp.concatenate(mixed_cols, axis=1)
    sg = (u * mixed).astype(BF16)
    s_br = _dot(sg, w_brs_ref[...])

    ga = _dot(h, w_in_ref[:, OFF_GA:OFF_GA + D_MODEL])
    gs = _dot(h, w_in_ref[:, OFF_GS:OFF_GS + D_MODEL])
    merged = (_sigmoid(ga) * a_br + _sigmoid(gs) * s_br).astype(BF16)
    y = _dot(merged, w_out_ref[...])
    z = DEEPNORM_ALPHA * x + (1.0 + g1) * y
    o_ref[...] = _layer_norm(z) * ln1_ref[0:1, :] + ln1_ref[1:2, :]


def _resident(shape):
    ndim = len(shape)
    return pl.BlockSpec(shape, lambda b, j: (0,) * ndim, pipeline_mode=pl.Buffered(1))


def _token_mix(x, mod_l, w_in, w_bra, w_brs, w_out, bias, sink_col, sgw_pair, sgb_full, sgln, ln1):
    B, S, D = x.shape
    return pl.pallas_call(
        _token_mix_kernel,
        grid=(B, S // TOKEN_TILE),
        in_specs=[
            pl.BlockSpec((None, TOKEN_TILE, D), lambda b, j: (b, j, 0)),
            pl.BlockSpec((None, 6, D), lambda b, j: (b, 0, 0)),
            _resident(w_in.shape),
            _resident(w_bra.shape),
            _resident(w_brs.shape),
            _resident(w_out.shape),
            _resident(bias.shape),
            _resident(sink_col.shape),
            _resident(sgw_pair.shape),
            _resident(sgb_full.shape),
            _resident(sgln.shape),
            _resident(ln1.shape),
        ],
        out_specs=pl.BlockSpec((None, TOKEN_TILE, D), lambda b, j: (b, j, 0)),
        out_shape=jax.ShapeDtypeStruct((B, S, D), F32),
        scratch_shapes=[pltpu.VMEM((BLOCK, KV_WIDTH), BF16), pltpu.VMEM((BLOCK, KV_WIDTH), BF16)],
        compiler_params=pltpu.CompilerParams(
            dimension_semantics=("arbitrary", "arbitrary"),
            vmem_limit_bytes=VMEM_LIMIT_BYTES),
        name="token_mix",
    )(x, mod_l, w_in, w_bra, w_brs, w_out, bias, sink_col, sgw_pair, sgb_full, sgln, ln1)


def _route(logits_t, rbias):
    rows = [logits_t[e:e + 1, :] for e in range(N_EXPERTS)]
    m = functools.reduce(jnp.maximum, rows)
    ex = [jnp.exp(r - m) for r in rows]
    denom = functools.reduce(lambda a, b: a + b, ex)
    probs = [e_ / denom for e_ in ex]
    sel = [probs[e] + rbias[e:e + 1, :] for e in range(N_EXPERTS)]

    def beats(a, b, a_first):
        return (a >= b) if a_first else (a > b)

    chosen = []
    score = []
    for g in range(N_EXPERT_GROUPS):
        members = list(range(g * EXPERTS_PER_GROUP, (g + 1) * EXPERTS_PER_GROUP))
        total = None
        for i in members:
            rank = None
            for o in members:
                if o == i:
                    continue
                ahead = beats(sel[o], sel[i], o < i).astype(F32)
                rank = ahead if rank is None else rank + ahead
            pick = rank < TOP_K
            chosen.append(pick)
            term = jnp.where(pick, sel[i], 0.0)
            total = term if total is None else total + term
        score.append(total)
    best = []
    for g in range(N_EXPERT_GROUPS):
        is_best = None
        for o in range(N_EXPERT_GROUPS):
            if o == g:
                continue
            wins = beats(score[g], score[o], g < o)
            is_best = wins if is_best is None else jnp.logical_and(is_best, wins)
        best.append(is_best)
    picked = [jnp.where(jnp.logical_and(chosen[e], best[e // EXPERTS_PER_GROUP]), probs[e], 0.0)
              for e in range(N_EXPERTS)]
    norm = functools.reduce(lambda a, b: a + b, picked)
    return jnp.concatenate([p_ / norm for p_ in picked], axis=0)


def _moe_kernel(x_ref, mod_ref, wr_ref, rb_ref, wgu_ref, wd_ref, ln2_ref, o_ref, hid_ref):
    x = x_ref[...]
    sh2 = mod_ref[3:4, :]
    sc2 = mod_ref[4:5, :]
    g2 = mod_ref[5:6, :]
    h = (_layer_norm(x) * (1.0 + sc2) + sh2).astype(BF16)

    logits_t = _dot_nt(wr_ref[...], h)
    comb_t = _route(logits_t, rb_ref[...])
    pad = jnp.zeros((LANES - N_EXPERTS, TOKEN_TILE), F32)
    comb = jnp.transpose(jnp.concatenate([comb_t, pad], axis=0))

    for e in range(N_EXPERTS):
        gu = _dot(h, wgu_ref[e])
        hid = _silu(gu[:, :D_FF_EXPERT]) * gu[:, D_FF_EXPERT:] * comb[:, e:e + 1]
        hid_ref[:, e * D_FF_EXPERT:(e + 1) * D_FF_EXPERT] = hid.astype(BF16)
    y = _dot(hid_ref[...], wd_ref[...])
    z = DEEPNORM_ALPHA * x + (1.0 + g2) * y
    o_ref[...] = _layer_norm(z) * ln2_ref[0:1, :] + ln2_ref[1:2, :]


def _moe(x, mod_l, wr_t, rbias, wgu, wd, ln2):
    B, S, D = x.shape
    return pl.pallas_call(
        _moe_kernel,
        grid=(B, S // TOKEN_TILE),
        in_specs=[
            pl.BlockSpec((None, TOKEN_TILE, D), lambda b, j: (b, j, 0)),
            pl.BlockSpec((None, 6, D), lambda b, j: (b, 0, 0)),
            _resident(wr_t.shape),
            _resident(rbias.shape),
            _resident(wgu.shape),
            _resident(wd.shape),
            _resident(ln2.shape),
        ],
        out_specs=pl.BlockSpec((None, TOKEN_TILE, D), lambda b, j: (b, j, 0)),
        out_shape=jax.ShapeDtypeStruct((B, S, D), F32),
        scratch_shapes=[pltpu.VMEM((TOKEN_TILE, N_EXPERTS * D_FF_EXPERT), BF16)],
        compiler_params=pltpu.CompilerParams(
            dimension_semantics=("arbitrary", "arbitrary"),
            vmem_limit_bytes=VMEM_LIMIT_BYTES),
        name="grouped_moe",
    )(x, mod_l, wr_t, rbias, wgu, wd, ln2)


def kernel(x, c, rel_bias_table, w_router, router_bias, w_ada, b_ada, w_in, sinks, sg_w, sg_b, sg_ln_g, sg_ln_b,
           w_br_attn, w_br_sg, w_out, ln1_g, ln1_b, w_gate, w_up, w_down, ln2_g, ln2_b):
    L = w_in.shape[0]
    B = x.shape[0]
    mod = _ada_modulation(c, w_ada, b_ada).reshape(L, B, 6, D_MODEL)
    bias = _band_bias(rel_bias_table)
    bias = bias.reshape(N_KV_HEADS, Q_PER_KV * BLOCK, 2 * BLOCK)

    wr_t = jnp.transpose(w_router).astype(BF16)
    rbias = router_bias.reshape(N_EXPERTS, 1)
    for l in range(L):
        sink_col = jnp.repeat(sinks[l].reshape(N_KV_HEADS, Q_PER_KV), BLOCK, axis=1).reshape(
            N_KV_HEADS, Q_PER_KV * BLOCK, 1)
        sgw_pair = jnp.concatenate([sg_w[l, 0::2], sg_w[l, 1::2]], axis=-1).astype(BF16)
        sgb_full = jnp.repeat(jnp.transpose(sg_b[l]), SG_GROUP_DIM, axis=1)
        sgln = jnp.stack([sg_ln_g[l], sg_ln_b[l]])
        ln1 = jnp.stack([ln1_g[l], ln1_b[l]])
        ln2 = jnp.stack([ln2_g[l], ln2_b[l]])
        x = _token_mix(x, mod[l], w_in[l].astype(BF16), w_br_attn[l].astype(BF16), w_br_sg[l].astype(BF16),
                       w_out[l].astype(BF16), bias, sink_col, sgw_pair, sgb_full, sgln, ln1)
        wgu = jnp.concatenate([w_gate[l], w_up[l]], axis=-1).astype(BF16)
        wd = w_down[l].reshape(N_EXPERTS * D_FF_EXPERT, D_MODEL).astype(BF16)
        x = _moe(x, mod[l], wr_t, rbias, wgu, wd, ln2)
    return x
```

```python
import functools
import math

import numpy as np
import jax
import jax.numpy as jnp
from jax import lax
from jax.experimental import pallas as pl
from jax.experimental.pallas import tpu as pltpu

D_MODEL = 1024
DEPTH = 4
N_Q_HEADS = 8
N_KV_HEADS = 2
HEAD_DIM = 64
Q_PER_KV = N_Q_HEADS // N_KV_HEADS
WINDOW = 128
BLOCK = WINDOW
ATTN_WIDTH = N_Q_HEADS * HEAD_DIM
KV_WIDTH = N_KV_HEADS * HEAD_DIM
N_BUCKETS = 32
MAX_DISTANCE = 128
N_SG_GROUPS = 8
SG_GROUP_DIM = 64
SG_CHUNK = 128
SG_WIDTH = N_SG_GROUPS * SG_GROUP_DIM
IN_WIDTH = ATTN_WIDTH + 2 * KV_WIDTH + 2 * SG_WIDTH + 2 * D_MODEL
N_EXPERTS = 16
N_EXPERT_GROUPS = 4
EXPERTS_PER_GROUP = N_EXPERTS // N_EXPERT_GROUPS
TOP_K = 2
D_FF_EXPERT = 256
DEEPNORM_ALPHA = (2 * DEPTH) ** 0.25
LN_EPS = 1e-5
NEG_INF = -1e30

OFF_Q = 0
OFF_KV = ATTN_WIDTH
OFF_SU = ATTN_WIDTH + 2 * KV_WIDTH
OFF_SV = OFF_SU + SG_WIDTH
OFF_GA = OFF_SV + SG_WIDTH
OFF_GS = OFF_GA + D_MODEL

LANES = 128
TOKEN_TILE = 512
BLOCKS_PER_TILE = TOKEN_TILE // BLOCK
VMEM_LIMIT_BYTES = 56 * 1024 * 1024

BF16 = jnp.bfloat16
F32 = jnp.float32


def _layer_norm(x):
    mu = jnp.mean(x, axis=-1, keepdims=True)
    xc = x - mu
    var = jnp.mean(xc * xc, axis=-1, keepdims=True)
    return xc * lax.rsqrt(var + LN_EPS)


def _dot(a, b):
    return jnp.dot(a, b, preferred_element_type=F32)


def _dot_nt(a, b):
    return lax.dot_general(a, b, (((1,), (1,)), ((), ())), preferred_element_type=F32)


def _gelu(x):
    return 0.5 * x * (1.0 + lax.erf(x * (1.0 / math.sqrt(2.0))))


def _sigmoid(x):
    return 1.0 / (1.0 + jnp.exp(-x))


def _silu(x):
    return x * _sigmoid(x)


ADA_TILE = 1536


def _ada_kernel(c_ref, w_ref, b_ref, o_ref):
    cond = _silu(c_ref[...]).astype(BF16)
    o_ref[...] = _dot(cond, w_ref[...].astype(BF16)) + b_ref[...]


def _ada_modulation(c, w_ada, b_ada):
    L, D, N = w_ada.shape
    B = c.shape[0]
    return pl.pallas_call(
        _ada_kernel,
        grid=(L, N // ADA_TILE),
        in_specs=[
            pl.BlockSpec((B, D), lambda l, n: (0, 0)),
            pl.BlockSpec((None, D, ADA_TILE), lambda l, n: (l, 0, n)),
            pl.BlockSpec((None, 1, ADA_TILE), lambda l, n: (l, 0, n)),
        ],
        out_specs=pl.BlockSpec((None, B, ADA_TILE), lambda l, n: (l, 0, n)),
        out_shape=jax.ShapeDtypeStruct((L, B, N), F32),
        compiler_params=pltpu.CompilerParams(
            dimension_semantics=("arbitrary", "arbitrary"),
            vmem_limit_bytes=VMEM_LIMIT_BYTES),
        name="ada_modulation",
    )(c, w_ada, b_ada.reshape(L, 1, N))


def _band_buckets():
    qi = np.arange(BLOCK)[:, None]
    kj = np.arange(2 * BLOCK)[None, :]
    rel = qi + BLOCK - kj
    n = np.maximum(rel, 0)
    max_exact = N_BUCKETS // 2
    nf = np.maximum(n, 1).astype(np.float32)
    large = max_exact + (np.log(nf / np.float32(max_exact)) / np.float32(math.log(MAX_DISTANCE / max_exact))
                         * np.float32(N_BUCKETS - max_exact)).astype(np.int32)
    large = np.minimum(large, N_BUCKETS - 1)
    bucket = np.where(n < max_exact, n, large).astype(np.int32)
    in_window = ((rel >= 0) & (rel < WINDOW)).astype(np.int32)
    return bucket, in_window


def _bias_kernel(table_ref, bucket_ref, window_ref, o_ref):
    bucket = bucket_ref[...]
    in_window = window_ref[...] > 0
    for head in range(N_Q_HEADS):
        acc = jnp.zeros((BLOCK, 2 * BLOCK), F32)
        for b in range(N_BUCKETS):
            acc = jnp.where(bucket == b, table_ref[b, head], acc)
        o_ref[head] = jnp.where(in_window, acc, NEG_INF)


def _band_bias(rel_bias_table):
    bucket, in_window = _band_buckets()
    return pl.pallas_call(
        _bias_kernel,
        in_specs=[
            pl.BlockSpec(memory_space=pltpu.SMEM),
            pl.BlockSpec(memory_space=pltpu.VMEM),
            pl.BlockSpec(memory_space=pltpu.VMEM),
        ],
        out_specs=pl.BlockSpec(memory_space=pltpu.VMEM),
        out_shape=jax.ShapeDtypeStruct((N_Q_HEADS, BLOCK, 2 * BLOCK), F32),
        name="band_bias",
    )(rel_bias_table, jnp.asarray(bucket), jnp.asarray(in_window))


def _token_mix_kernel(x_ref, mod_ref, w_in_ref, w_bra_ref, w_brs_ref, w_out_ref, bias_ref, sink_ref,
                      sgw_ref, sgb_ref, sgln_ref, ln1_ref, o_ref, kprev_ref, vprev_ref):
    j = pl.program_id(1)

    @pl.when(j == 0)
    def _():
        kprev_ref[...] = jnp.zeros_like(kprev_ref)
        vprev_ref[...] = jnp.zeros_like(vprev_ref)

    x = x_ref[...]
    sh1 = mod_ref[0:1, :]
    sc1 = mod_ref[1:2, :]
    g1 = mod_ref[2:3, :]
    h = (_layer_norm(x) * (1.0 + sc1) + sh1).astype(BF16)

    lane = lax.broadcasted_iota(jnp.int32, (1, LANES), 1)
    lo = lane < HEAD_DIM

    q = (_dot(h, w_in_ref[:, OFF_Q:OFF_Q + ATTN_WIDTH]) * (HEAD_DIM ** -0.5)).astype(BF16)
    kv = _dot(h, w_in_ref[:, OFF_KV:OFF_KV + 2 * KV_WIDTH]).astype(BF16)
    k_full = jnp.concatenate([kprev_ref[...], kv[:, :KV_WIDTH]], axis=0)
    v_full = jnp.concatenate([vprev_ref[...], kv[:, KV_WIDTH:]], axis=0)
    kprev_ref[...] = kv[TOKEN_TILE - BLOCK:, :KV_WIDTH]
    vprev_ref[...] = kv[TOKEN_TILE - BLOCK:, KV_WIDTH:]

    k_sw = pltpu.roll(k_full, HEAD_DIM, 1)
    k_dup = (jnp.where(lo, k_full, k_sw), jnp.where(lo, k_sw, k_full))
    one = jnp.ones((), BF16)
    v_ext = (jnp.where(lo, v_full, one), jnp.where(lo, one, v_full))

    key_in_prev = lax.broadcasted_iota(jnp.int32, (1, 2 * BLOCK), 1) < BLOCK
    zero = jnp.zeros((), BF16)
    attn_blocks = []
    for n in range(BLOCKS_PER_TILE):
        r0 = n * BLOCK
        slabs = []
        for hk in range(N_KV_HEADS):
            qa = q[r0:r0 + BLOCK, hk * 256:hk * 256 + LANES]
            qb = q[r0:r0 + BLOCK, hk * 256 + LANES:hk * 256 + 2 * LANES]
            qs = jnp.concatenate([jnp.where(lo, qa, zero), jnp.where(lo, zero, qa),
                                  jnp.where(lo, qb, zero), jnp.where(lo, zero, qb)], axis=0)
            s = _dot_nt(qs, k_dup[hk][r0:r0 + 2 * BLOCK]) + bias_ref[hk]
            if n == 0:
                s = jnp.where(jnp.logical_and(j == 0, key_in_prev), NEG_INF, s)
            sink = sink_ref[hk]
            m = jnp.maximum(jnp.max(s, axis=-1, keepdims=True), sink)
            p = jnp.exp(s - m).astype(BF16)
            pv = _dot(p, v_ext[hk][r0:r0 + 2 * BLOCK])
            pv_sw = pltpu.roll(pv, HEAD_DIM, 1)
            row_sum = jnp.where(lo, pv_sw, pv) if hk == 0 else jnp.where(lo, pv, pv_sw)
            out = pv / (row_sum + jnp.exp(sink - m))
            out_sw = pltpu.roll(out, HEAD_DIM, 1)
            for pair in range(2):
                first = slice(2 * pair * BLOCK, (2 * pair + 1) * BLOCK)
                second = slice((2 * pair + 1) * BLOCK, (2 * pair + 2) * BLOCK)
                if hk == 0:
                    slabs.append(jnp.where(lo, out[first], out_sw[second]))
                else:
                    slabs.append(jnp.where(lo, out_sw[first], out[second]))
        attn_blocks.append(jnp.concatenate(slabs, axis=1).astype(BF16))
    attn = jnp.concatenate(attn_blocks, axis=0)
    a_br = _dot(attn, w_bra_ref[...])

    su = _dot(h, w_in_ref[:, OFF_SU:OFF_SU + SG_WIDTH])
    sv = _dot(h, w_in_ref[:, OFF_SV:OFF_SV + SG_WIDTH])
    u = _gelu(su)
    vn = (_layer_norm(_gelu(sv)) * sgln_ref[0:1, :] + sgln_ref[1:2, :]).astype(BF16)
    row = lax.broadcasted_iota(jnp.int32, (SG_CHUNK, 2 * SG_CHUNK), 0)
    col = lax.broadcasted_iota(jnp.int32, (SG_CHUNK, 2 * SG_CHUNK), 1)
    causal = (col & (SG_CHUNK - 1)) <= row
    mixed_cols = []
    for s_idx in range(SG_WIDTH // LANES):
        w_pair = jnp.where(causal, sgw_ref[s_idx], zero)
        b_slab = sgb_ref[:, s_idx * LANES:(s_idx + 1) * LANES]
        rows = []
        for c in range(TOKEN_TILE // SG_CHUNK):
            slab = vn[c * SG_CHUNK:(c + 1) * SG_CHUNK, s_idx * LANES:(s_idx + 1) * LANES]
            rhs = jnp.concatenate([jnp.where(lo, slab, zero), jnp.where(lo, zero, slab)], axis=0)
            rows.append(_dot(w_pair, rhs) + b_slab)
        mixed_cols.append(jnp.concatenate(rows, axis=0))
    mixed = jnp.concatenate(mixed_cols, axis=1)
    sg = (u * mixed).astype(BF16)
    s_br = _dot(sg, w_brs_ref[...])

    ga = _dot(h, w_in_ref[:, OFF_GA:OFF_GA + D_MODEL])
    gs = _dot(h, w_in_ref[:, OFF_GS:OFF_GS + D_MODEL])
    merged = (_sigmoid(ga) * a_br + _sigmoid(gs) * s_br).astype(BF16)
    y = _dot(merged, w_out_ref[...])
    z = DEEPNORM_ALPHA * x + (1.0 + g1) * y
    o_ref[...] = _layer_norm(z) * ln1_ref[0:1, :] + ln1_ref[1:2, :]


def _resident(shape):
    ndim = len(shape)
    return pl.BlockSpec(shape, lambda b, j: (0,) * ndim, pipeline_mode=pl.Buffered(1))


def _token_mix(x, mod_l, w_in, w_bra, w_brs, w_out, bias, sink_col, sgw_pair, sgb_full, sgln, ln1):
    B, S, D = x.shape
    return pl.pallas_call(
        _token_mix_kernel,
        grid=(B, S // TOKEN_TILE),
        in_specs=[
            pl.BlockSpec((None, TOKEN_TILE, D), lambda b, j: (b, j, 0)),
            pl.BlockSpec((None, 6, D), lambda b, j: (b, 0, 0)),
            _resident(w_in.shape),
            _resident(w_bra.shape),
            _resident(w_brs.shape),
            _resident(w_out.shape),
            _resident(bias.shape),
            _resident(sink_col.shape),
            _resident(sgw_pair.shape),
            _resident(sgb_full.shape),
            _resident(sgln.shape),
            _resident(ln1.shape),
        ],
        out_specs=pl.BlockSpec((None, TOKEN_TILE, D), lambda b, j: (b, j, 0)),
        out_shape=jax.ShapeDtypeStruct((B, S, D), F32),
        scratch_shapes=[pltpu.VMEM((BLOCK, KV_WIDTH), BF16), pltpu.VMEM((BLOCK, KV_WIDTH), BF16)],
        compiler_params=pltpu.CompilerParams(
            dimension_semantics=("arbitrary", "arbitrary"),
            vmem_limit_bytes=VMEM_LIMIT_BYTES),
        name="token_mix",
    )(x, mod_l, w_in, w_bra, w_brs, w_out, bias, sink_col, sgw_pair, sgb_full, sgln, ln1)


def _route(logits_t, rbias):
    rows = [logits_t[e:e + 1, :] for e in range(N_EXPERTS)]
    m = functools.reduce(jnp.maximum, rows)
    ex = [jnp.exp(r - m) for r in rows]
    denom = functools.reduce(lambda a, b: a + b, ex)
    probs = [e_ / denom for e_ in ex]
    sel = [probs[e] + rbias[e:e + 1, :] for e in range(N_EXPERTS)]

    def beats(a, b, a_first):
        return (a >= b) if a_first else (a > b)

    chosen = []
    score = []
    for g in range(N_EXPERT_GROUPS):
        members = list(range(g * EXPERTS_PER_GROUP, (g + 1) * EXPERTS_PER_GROUP))
        total = None
        for i in members:
            rank = None
            for o in members:
                if o == i:
                    continue
                ahead = beats(sel[o], sel[i], o < i).astype(F32)
                rank = ahead if rank is None else rank + ahead
            pick = rank < TOP_K
            chosen.append(pick)
            term = jnp.where(pick, sel[i], 0.0)
            total = term if total is None else total + term
        score.append(total)
    best = []
    for g in range(N_EXPERT_GROUPS):
        is_best = None
        for o in range(N_EXPERT_GROUPS):
            if o == g:
                continue
            wins = beats(score[g], score[o], g < o)
            is_best = wins if is_best is None else jnp.logical_and(is_best, wins)
        best.append(is_best)
    picked = [jnp.where(jnp.logical_and(chosen[e], best[e // EXPERTS_PER_GROUP]), probs[e], 0.0)
              for e in range(N_EXPERTS)]
    norm = functools.reduce(lambda a, b: a + b, picked)
    return [p_ / norm for p_ in picked], best


SUB_TILE = 128
SEG_ALIGN = 16
SORT_SLOTS = TOKEN_TILE + N_EXPERT_GROUPS * SEG_ALIGN
SORT_ROWS = SORT_SLOTS + SUB_TILE
UNSORT_COLS = -(-SORT_SLOTS // LANES) * LANES
POS_LANE = N_EXPERTS


def _moe_kernel(x_ref, mod_ref, wr_ref, rb_ref, wg_ref, wu_ref, wd_ref, ln2_ref, o_ref, hs_ref, cs_ref, ys_ref):
    T = TOKEN_TILE

    @pl.when(jnp.logical_and(pl.program_id(0) == 0, pl.program_id(1) == 0))
    def _():
        hs_ref[SORT_SLOTS:, :] = jnp.zeros((SORT_ROWS - SORT_SLOTS, D_MODEL), BF16)
        cs_ref[SORT_SLOTS:, :] = jnp.zeros((SORT_ROWS - SORT_SLOTS, LANES), F32)

    x = x_ref[...]
    sh2 = mod_ref[3:4, :]
    sc2 = mod_ref[4:5, :]
    g2 = mod_ref[5:6, :]
    h = (_layer_norm(x) * (1.0 + sc2) + sh2).astype(BF16)

    logits_t = _dot_nt(wr_ref[...], h)
    comb_rows, best = _route(logits_t, rb_ref[...])

    onehot = jnp.concatenate([b.astype(F32) for b in best] + [jnp.zeros((8 - N_EXPERT_GROUPS, T), F32)], axis=0)
    before = lax.broadcasted_iota(jnp.int32, (T, T), 0) < lax.broadcasted_iota(jnp.int32, (T, T), 1)
    rank = _dot(onehot.astype(BF16), before.astype(BF16))
    counts = [jnp.sum(onehot[g:g + 1, :]).astype(jnp.int32) for g in range(N_EXPERT_GROUPS)]
    bases = []
    acc = jnp.int32(0)
    for g in range(N_EXPERT_GROUPS):
        bases.append(acc)
        acc = acc + ((counts[g] + (SEG_ALIGN - 1)) // SEG_ALIGN) * SEG_ALIGN
    pos_f = functools.reduce(lambda a, b: a + b,
                             [onehot[g:g + 1, :] * (rank[g:g + 1, :] + bases[g].astype(F32))
                              for g in range(N_EXPERT_GROUPS)])
    pos = pos_f.astype(jnp.int32)

    sort_mat = (lax.broadcasted_iota(jnp.int32, (SORT_SLOTS, T), 0) == pos).astype(BF16)
    hs_ref[:SORT_SLOTS, :] = _dot(sort_mat, h).astype(BF16)

    tok_t = jnp.concatenate(comb_rows + [pos_f, jnp.zeros((LANES - N_EXPERTS - 1, T), F32)], axis=0)
    tok = jnp.transpose(tok_t)
    hi = tok.astype(BF16)
    r1 = tok - hi.astype(F32)
    mid = r1.astype(BF16)
    lo = (r1 - mid.astype(F32)).astype(BF16)
    parts = _dot(sort_mat, jnp.concatenate([hi, mid, lo], axis=1))
    cs_ref[:SORT_SLOTS, :] = parts[:, :LANES] + parts[:, LANES:2 * LANES] + parts[:, 2 * LANES:]

    ys_ref[T:, :] = jnp.zeros((SORT_ROWS - T, D_MODEL), F32)

    for g in range(N_EXPERT_GROUPS):
        def body(k, carry, g=g):
            start = pl.multiple_of(bases[g] + k * SUB_TILE, SEG_ALIGN)
            hs = hs_ref[pl.ds(start, SUB_TILE), :]
            cw = cs_ref[pl.ds(start, SUB_TILE), :]
            gate = _dot(hs, wg_ref[g])
            up = _dot(hs, wu_ref[g])
            scale = jnp.concatenate(
                [jnp.broadcast_to(cw[:, g * EXPERTS_PER_GROUP + i:g * EXPERTS_PER_GROUP + i + 1],
                                  (SUB_TILE, D_FF_EXPERT)) for i in range(EXPERTS_PER_GROUP)], axis=1)
            hid = (_silu(gate) * up * scale).astype(BF16)
            ys_ref[pl.ds(start, SUB_TILE), :] = _dot(hid, wd_ref[g])
            return carry
        lax.fori_loop(0, (counts[g] + (SUB_TILE - 1)) // SUB_TILE, body, 0)

    unsort_mat = (tok[:, POS_LANE:POS_LANE + 1].astype(jnp.int32)
                  == lax.broadcasted_iota(jnp.int32, (T, UNSORT_COLS), 1)).astype(BF16)
    y = _dot(unsort_mat, ys_ref[:UNSORT_COLS, :].astype(BF16))
    z = DEEPNORM_ALPHA * x + (1.0 + g2) * y
    o_ref[...] = _layer_norm(z) * ln2_ref[0:1, :] + ln2_ref[1:2, :]


def _moe(x, mod_l, wr_t, rbias, wg, wu, wd, ln2):
    B, S, D = x.shape
    return pl.pallas_call(
        _moe_kernel,
        grid=(B, S // TOKEN_TILE),
        in_specs=[
            pl.BlockSpec((None, TOKEN_TILE, D), lambda b, j: (b, j, 0)),
            pl.BlockSpec((None, 6, D), lambda b, j: (b, 0, 0)),
            _resident(wr_t.shape),
            _resident(rbias.shape),
            _resident(wg.shape),
            _resident(wu.shape),
            _resident(wd.shape),
            _resident(ln2.shape),
        ],
        out_specs=pl.BlockSpec((None, TOKEN_TILE, D), lambda b, j: (b, j, 0)),
        out_shape=jax.ShapeDtypeStruct((B, S, D), F32),
        scratch_shapes=[pltpu.VMEM((SORT_ROWS, D_MODEL), BF16),
                        pltpu.VMEM((SORT_ROWS, LANES), F32),
                        pltpu.VMEM((SORT_ROWS, D_MODEL), F32)],
        compiler_params=pltpu.CompilerParams(
            dimension_semantics=("arbitrary", "arbitrary"),
            vmem_limit_bytes=VMEM_LIMIT_BYTES),
        name="grouped_moe",
    )(x, mod_l, wr_t, rbias, wg, wu, wd, ln2)


def _group_cols(w):
    w = w.reshape(N_EXPERT_GROUPS, EXPERTS_PER_GROUP, D_MODEL, D_FF_EXPERT)
    w = jnp.transpose(w, (0, 2, 1, 3))
    return w.reshape(N_EXPERT_GROUPS, D_MODEL, EXPERTS_PER_GROUP * D_FF_EXPERT).astype(BF16)


def kernel(x, c, rel_bias_table, w_router, router_bias, w_ada, b_ada, w_in, sinks, sg_w, sg_b, sg_ln_g, sg_ln_b,
           w_br_attn, w_br_sg, w_out, ln1_g, ln1_b, w_gate, w_up, w_down, ln2_g, ln2_b):
    L = w_in.shape[0]
    B = x.shape[0]
    mod = _ada_modulation(c, w_ada, b_ada).reshape(L, B, 6, D_MODEL)
    bias = _band_bias(rel_bias_table)
    bias = bias.reshape(N_KV_HEADS, Q_PER_KV * BLOCK, 2 * BLOCK)

    wr_t = jnp.transpose(w_router).astype(BF16)
    rbias = router_bias.reshape(N_EXPERTS, 1)
    for l in range(L):
        sink_col = jnp.repeat(sinks[l].reshape(N_KV_HEADS, Q_PER_KV), BLOCK, axis=1).reshape(
            N_KV_HEADS, Q_PER_KV * BLOCK, 1)
        sgw_pair = jnp.concatenate([sg_w[l, 0::2], sg_w[l, 1::2]], axis=-1).astype(BF16)
        sgb_full = jnp.repeat(jnp.transpose(sg_b[l]), SG_GROUP_DIM, axis=1)
        sgln = jnp.stack([sg_ln_g[l], sg_ln_b[l]])
        ln1 = jnp.stack([ln1_g[l], ln1_b[l]])
        ln2 = jnp.stack([ln2_g[l], ln2_b[l]])
        x = _token_mix(x, mod[l], w_in[l].astype(BF16), w_br_attn[l].astype(BF16), w_br_sg[l].astype(BF16),
                       w_out[l].astype(BF16), bias, sink_col, sgw_pair, sgb_full, sgln, ln1)
        x = _moe(x, mod[l], wr_t, rbias, _group_cols(w_gate[l]), _group_cols(w_up[l]),
                 w_down[l].reshape(N_EXPERT_GROUPS, EXPERTS_PER_GROUP * D_FF_EXPERT, D_MODEL).astype(BF16), ln2)
    return x
```

```python
import functools
import math

import numpy as np
import jax
import jax.numpy as jnp
from jax import lax
from jax.experimental import pallas as pl
from jax.experimental.pallas import tpu as pltpu

D_MODEL = 1024
DEPTH = 4
N_Q_HEADS = 8
N_KV_HEADS = 2
HEAD_DIM = 64
Q_PER_KV = N_Q_HEADS // N_KV_HEADS
WINDOW = 128
BLOCK = WINDOW
ATTN_WIDTH = N_Q_HEADS * HEAD_DIM
KV_WIDTH = N_KV_HEADS * HEAD_DIM
N_BUCKETS = 32
MAX_DISTANCE = 128
N_SG_GROUPS = 8
SG_GROUP_DIM = 64
SG_CHUNK = 128
SG_WIDTH = N_SG_GROUPS * SG_GROUP_DIM
IN_WIDTH = ATTN_WIDTH + 2 * KV_WIDTH + 2 * SG_WIDTH + 2 * D_MODEL
N_EXPERTS = 16
N_EXPERT_GROUPS = 4
EXPERTS_PER_GROUP = N_EXPERTS // N_EXPERT_GROUPS
TOP_K = 2
D_FF_EXPERT = 256
DEEPNORM_ALPHA = (2 * DEPTH) ** 0.25
LN_EPS = 1e-5
NEG_INF = -1e30

OFF_Q = 0
OFF_KV = ATTN_WIDTH
OFF_SU = ATTN_WIDTH + 2 * KV_WIDTH
OFF_SV = OFF_SU + SG_WIDTH
OFF_GA = OFF_SV + SG_WIDTH
OFF_GS = OFF_GA + D_MODEL

LANES = 128
TOKEN_TILE = 512
BLOCKS_PER_TILE = TOKEN_TILE // BLOCK
REST_SLAB = 256
VMEM_LIMIT_BYTES = 56 * 1024 * 1024

BF16 = jnp.bfloat16
F32 = jnp.float32


def _layer_norm(x):
    mu = jnp.mean(x, axis=-1, keepdims=True)
    xc = x - mu
    var = jnp.mean(xc * xc, axis=-1, keepdims=True)
    return xc * lax.rsqrt(var + LN_EPS)


def _dot(a, b):
    return jnp.dot(a, b, preferred_element_type=F32)


def _dot_nt(a, b):
    return lax.dot_general(a, b, (((1,), (1,)), ((), ())), preferred_element_type=F32)


def _gelu(x):
    return 0.5 * x * (1.0 + lax.erf(x * (1.0 / math.sqrt(2.0))))


def _sigmoid(x):
    return 0.5 * (jnp.tanh(0.5 * x) + 1.0)


def _silu(x):
    return x * _sigmoid(x)


ADA_TILE = 1536


def _ada_kernel(c_ref, w_ref, b_ref, o_ref):
    cond = _silu(c_ref[...]).astype(BF16)
    o_ref[...] = _dot(cond, w_ref[...].astype(BF16)) + b_ref[...]


def _ada_modulation(c, w_ada, b_ada):
    L, D, N = w_ada.shape
    B = c.shape[0]
    return pl.pallas_call(
        _ada_kernel,
        grid=(L, N // ADA_TILE),
        in_specs=[
            pl.BlockSpec((B, D), lambda l, n: (0, 0)),
            pl.BlockSpec((None, D, ADA_TILE), lambda l, n: (l, 0, n)),
            pl.BlockSpec((None, 1, ADA_TILE), lambda l, n: (l, 0, n)),
        ],
        out_specs=pl.BlockSpec((None, B, ADA_TILE), lambda l, n: (l, 0, n)),
        out_shape=jax.ShapeDtypeStruct((L, B, N), F32),
        compiler_params=pltpu.CompilerParams(
            dimension_semantics=("arbitrary", "arbitrary"),
            vmem_limit_bytes=VMEM_LIMIT_BYTES),
        name="ada_modulation",
    )(c, w_ada, b_ada.reshape(L, 1, N))


def _band_buckets():
    qi = np.arange(BLOCK)[:, None]
    kj = np.arange(2 * BLOCK)[None, :]
    rel = qi + BLOCK - kj
    n = np.maximum(rel, 0)
    max_exact = N_BUCKETS // 2
    nf = np.maximum(n, 1).astype(np.float32)
    large = max_exact + (np.log(nf / np.float32(max_exact)) / np.float32(math.log(MAX_DISTANCE / max_exact))
                         * np.float32(N_BUCKETS - max_exact)).astype(np.int32)
    large = np.minimum(large, N_BUCKETS - 1)
    bucket = np.where(n < max_exact, n, large).astype(np.int32)
    in_window = ((rel >= 0) & (rel < WINDOW)).astype(np.int32)
    return bucket, in_window


def _bias_kernel(table_ref, bucket_ref, window_ref, o_ref):
    bucket = bucket_ref[...]
    in_window = window_ref[...] > 0
    for head in range(N_Q_HEADS):
        acc = jnp.zeros((BLOCK, 2 * BLOCK), F32)
        for b in range(N_BUCKETS):
            acc = jnp.where(bucket == b, table_ref[b, head], acc)
        o_ref[head] = jnp.where(in_window, acc, NEG_INF)


def _band_bias(rel_bias_table):
    bucket, in_window = _band_buckets()
    return pl.pallas_call(
        _bias_kernel,
        in_specs=[
            pl.BlockSpec(memory_space=pltpu.SMEM),
            pl.BlockSpec(memory_space=pltpu.VMEM),
            pl.BlockSpec(memory_space=pltpu.VMEM),
        ],
        out_specs=pl.BlockSpec(memory_space=pltpu.VMEM),
        out_shape=jax.ShapeDtypeStruct((N_Q_HEADS, BLOCK, 2 * BLOCK), F32),
        name="band_bias",
    )(rel_bias_table, jnp.asarray(bucket), jnp.asarray(in_window))


def _token_mix_kernel(x_ref, mod_ref, w_in_ref, w_bra_ref, w_brs_ref, w_out_ref, bias_ref, sink_ref,
                      sgw_ref, sgb_ref, sgln_ref, ln1_ref, o_ref, kprev_ref, vprev_ref):
    j = pl.program_id(1)

    @pl.when(j == 0)
    def _():
        kprev_ref[...] = jnp.zeros_like(kprev_ref)
        vprev_ref[...] = jnp.zeros_like(vprev_ref)

    x = x_ref[...]
    sh1 = mod_ref[0:1, :]
    sc1 = mod_ref[1:2, :]
    g1 = mod_ref[2:3, :]
    h = (_layer_norm(x) * (1.0 + sc1) + sh1).astype(BF16)

    lane = lax.broadcasted_iota(jnp.int32, (1, LANES), 1)
    lo = lane < HEAD_DIM

    q = (_dot(h, w_in_ref[:, OFF_Q:OFF_Q + ATTN_WIDTH]) * (HEAD_DIM ** -0.5)).astype(BF16)
    kv = _dot(h, w_in_ref[:, OFF_KV:OFF_KV + 2 * KV_WIDTH]).astype(BF16)
    k_full = jnp.concatenate([kprev_ref[...], kv[:, :KV_WIDTH]], axis=0)
    v_full = jnp.concatenate([vprev_ref[...], kv[:, KV_WIDTH:]], axis=0)
    kprev_ref[...] = kv[TOKEN_TILE - BLOCK:, :KV_WIDTH]
    vprev_ref[...] = kv[TOKEN_TILE - BLOCK:, KV_WIDTH:]

    k_sw = pltpu.roll(k_full, HEAD_DIM, 1)
    k_dup = (jnp.where(lo, k_full, k_sw), jnp.where(lo, k_sw, k_full))
    one = jnp.ones((), BF16)
    v_ext = (jnp.where(lo, v_full, one), jnp.where(lo, one, v_full))

    n_rest = (IN_WIDTH - OFF_SU) // REST_SLAB
    rest = []

    def project_rest(count):
        for _ in range(count):
            c0 = OFF_SU + len(rest) * REST_SLAB
            rest.append(_dot(h, w_in_ref[:, c0:c0 + REST_SLAB]))

    key_in_prev = lax.broadcasted_iota(jnp.int32, (1, 2 * BLOCK), 1) < BLOCK
    zero = jnp.zeros((), BF16)
    units = [(n, hk) for n in range(BLOCKS_PER_TILE) for hk in range(N_KV_HEADS)]

    def scores(n, hk):
        r0 = n * BLOCK
        qa = q[r0:r0 + BLOCK, hk * 256:hk * 256 + LANES]
        qb = q[r0:r0 + BLOCK, hk * 256 + LANES:hk * 256 + 2 * LANES]
        qs = jnp.concatenate([jnp.where(lo, qa, zero), jnp.where(lo, zero, qa),
                              jnp.where(lo, qb, zero), jnp.where(lo, zero, qb)], axis=0)
        s = _dot_nt(qs, k_dup[hk][r0:r0 + 2 * BLOCK]) + bias_ref[hk]
        if n == 0:
            s = jnp.where(jnp.logical_and(j == 0, key_in_prev), NEG_INF, s)
        return s

    def attend(s, n, hk):
        r0 = n * BLOCK
        sink = sink_ref[hk]
        m = jnp.maximum(jnp.max(s, axis=-1, keepdims=True), sink)
        p = jnp.exp(s - m).astype(BF16)
        pv = _dot(p, v_ext[hk][r0:r0 + 2 * BLOCK])
        pv_sw = pltpu.roll(pv, HEAD_DIM, 1)
        row_sum = jnp.where(lo, pv_sw, pv) if hk == 0 else jnp.where(lo, pv, pv_sw)
        out = pv / (row_sum + jnp.exp(sink - m))
        out_sw = pltpu.roll(out, HEAD_DIM, 1)
        slabs = []
        for pair in range(2):
            first = slice(2 * pair * BLOCK, (2 * pair + 1) * BLOCK)
            second = slice((2 * pair + 1) * BLOCK, (2 * pair + 2) * BLOCK)
            if hk == 0:
                slabs.append(jnp.where(lo, out[first], out_sw[second]))
            else:
                slabs.append(jnp.where(lo, out_sw[first], out[second]))
        return slabs

    slabs = {}
    s_next = scores(*units[0])
    for i, (n, hk) in enumerate(units):
        s_cur = s_next
        project_rest(1)
        if i + 1 < len(units):
            s_next = scores(*units[i + 1])
        if i % 2 == 1:
            project_rest(1)
        slabs[(n, hk)] = attend(s_cur, n, hk)
    attn = jnp.concatenate(
        [jnp.concatenate(slabs[(n, 0)] + slabs[(n, 1)], axis=1).astype(BF16) for n in range(BLOCKS_PER_TILE)],
        axis=0)
    a_br = _dot(attn, w_bra_ref[...])
    project_rest(n_rest - len(rest))
    rest = jnp.concatenate(rest, axis=1)
    su = rest[:, :SG_WIDTH]
    sv = rest[:, SG_WIDTH:2 * SG_WIDTH]
    ga = rest[:, 2 * SG_WIDTH:2 * SG_WIDTH + D_MODEL]
    gs = rest[:, 2 * SG_WIDTH + D_MODEL:]

    u = _gelu(su)
    vn = (_layer_norm(_gelu(sv)) * sgln_ref[0:1, :] + sgln_ref[1:2, :]).astype(BF16)
    row = lax.broadcasted_iota(jnp.int32, (SG_CHUNK, 2 * SG_CHUNK), 0)
    col = lax.broadcasted_iota(jnp.int32, (SG_CHUNK, 2 * SG_CHUNK), 1)
    causal = (col & (SG_CHUNK - 1)) <= row
    mixed_cols = []
    for s_idx in range(SG_WIDTH // LANES):
        w_pair = jnp.where(causal, sgw_ref[s_idx], zero)
        b_slab = sgb_ref[:, s_idx * LANES:(s_idx + 1) * LANES]
        rows = []
        for c in range(TOKEN_TILE // SG_CHUNK):
            slab = vn[c * SG_CHUNK:(c + 1) * SG_CHUNK, s_idx * LANES:(s_idx + 1) * LANES]
            rhs = jnp.concatenate([jnp.where(lo, slab, zero), jnp.where(lo, zero, slab)], axis=0)
            rows.append(_dot(w_pair, rhs) + b_slab)
        mixed_cols.append(jnp.concatenate(rows, axis=0))
    mixed = jnp.concatenate(mixed_cols, axis=1)
    sg = (u * mixed).astype(BF16)
    s_br = _dot(sg, w_brs_ref[...])

    merged = (_sigmoid(ga) * a_br + _sigmoid(gs) * s_br).astype(BF16)
    y = _dot(merged, w_out_ref[...])
    z = DEEPNORM_ALPHA * x + (1.0 + g1) * y
    o_ref[...] = _layer_norm(z) * ln1_ref[0:1, :] + ln1_ref[1:2, :]


def _resident(shape):
    ndim = len(shape)
    return pl.BlockSpec(shape, lambda b, j: (0,) * ndim, pipeline_mode=pl.Buffered(1))


def _token_mix(x, mod_l, w_in, w_bra, w_brs, w_out, bias, sink_col, sgw_pair, sgb_full, sgln, ln1):
    B, S, D = x.shape
    return pl.pallas_call(
        _token_mix_kernel,
        grid=(B, S // TOKEN_TILE),
        in_specs=[
            pl.BlockSpec((None, TOKEN_TILE, D), lambda b, j: (b, j, 0)),
            pl.BlockSpec((None, 6, D), lambda b, j: (b, 0, 0)),
            _resident(w_in.shape),
            _resident(w_bra.shape),
            _resident(w_brs.shape),
            _resident(w_out.shape),
            _resident(bias.shape),
            _resident(sink_col.shape),
            _resident(sgw_pair.shape),
            _resident(sgb_full.shape),
            _resident(sgln.shape),
            _resident(ln1.shape),
        ],
        out_specs=pl.BlockSpec((None, TOKEN_TILE, D), lambda b, j: (b, j, 0)),
        out_shape=jax.ShapeDtypeStruct((B, S, D), F32),
        scratch_shapes=[pltpu.VMEM((BLOCK, KV_WIDTH), BF16), pltpu.VMEM((BLOCK, KV_WIDTH), BF16)],
        compiler_params=pltpu.CompilerParams(
            dimension_semantics=("arbitrary", "arbitrary"),
            vmem_limit_bytes=VMEM_LIMIT_BYTES),
        name="token_mix",
    )(x, mod_l, w_in, w_bra, w_brs, w_out, bias, sink_col, sgw_pair, sgb_full, sgln, ln1)


def _route(logits_t, rbias):
    rows = [logits_t[e:e + 1, :] for e in range(N_EXPERTS)]
    m = functools.reduce(jnp.maximum, rows)
    ex = [jnp.exp(r - m) for r in rows]
    denom = functools.reduce(lambda a, b: a + b, ex)
    probs = [e_ / denom for e_ in ex]
    sel = [probs[e] + rbias[e:e + 1, :] for e in range(N_EXPERTS)]

    def beats(a, b, a_first):
        return (a >= b) if a_first else (a > b)

    chosen = []
    score = []
    for g in range(N_EXPERT_GROUPS):
        members = list(range(g * EXPERTS_PER_GROUP, (g + 1) * EXPERTS_PER_GROUP))
        total = None
        for i in members:
            rank = None
            for o in members:
                if o == i:
                    continue
                ahead = beats(sel[o], sel[i], o < i).astype(F32)
                rank = ahead if rank is None else rank + ahead
            pick = rank < TOP_K
            chosen.append(pick)
            term = jnp.where(pick, sel[i], 0.0)
            total = term if total is None else total + term
        score.append(total)
    best = []
    for g in range(N_EXPERT_GROUPS):
        is_best = None
        for o in range(N_EXPERT_GROUPS):
            if o == g:
                continue
            wins = beats(score[g], score[o], g < o)
            is_best = wins if is_best is None else jnp.logical_and(is_best, wins)
        best.append(is_best)
    picked = [jnp.where(jnp.logical_and(chosen[e], best[e // EXPERTS_PER_GROUP]), probs[e], 0.0)
              for e in range(N_EXPERTS)]
    norm = functools.reduce(lambda a, b: a + b, picked)
    return [p_ / norm for p_ in picked], best


SUB_TILE = 128
SEG_ALIGN = 16
SORT_SLOTS = TOKEN_TILE + N_EXPERT_GROUPS * SEG_ALIGN
SORT_ROWS = SORT_SLOTS + SUB_TILE
UNSORT_COLS = -(-SORT_SLOTS // LANES) * LANES
POS_LANE = N_EXPERTS


def _moe_kernel(x_ref, mod_ref, wr_ref, rb_ref, wg_ref, wu_ref, wd_ref, ln2_ref, o_ref, hs_ref, cs_ref, ys_ref):
    T = TOKEN_TILE

    @pl.when(jnp.logical_and(pl.program_id(0) == 0, pl.program_id(1) == 0))
    def _():
        hs_ref[SORT_SLOTS:, :] = jnp.zeros((SORT_ROWS - SORT_SLOTS, D_MODEL), BF16)
        cs_ref[SORT_SLOTS:, :] = jnp.zeros((SORT_ROWS - SORT_SLOTS, LANES), F32)

    x = x_ref[...]
    sh2 = mod_ref[3:4, :]
    sc2 = mod_ref[4:5, :]
    g2 = mod_ref[5:6, :]
    h = (_layer_norm(x) * (1.0 + sc2) + sh2).astype(BF16)

    logits_t = _dot_nt(wr_ref[...], h)
    comb_rows, best = _route(logits_t, rb_ref[...])

    onehot = jnp.concatenate([b.astype(F32) for b in best] + [jnp.zeros((8 - N_EXPERT_GROUPS, T), F32)], axis=0)
    before = lax.broadcasted_iota(jnp.int32, (T, T), 0) < lax.broadcasted_iota(jnp.int32, (T, T), 1)
    rank = _dot(onehot.astype(BF16), before.astype(BF16))
    counts = [jnp.sum(onehot[g:g + 1, :]).astype(jnp.int32) for g in range(N_EXPERT_GROUPS)]
    bases = []
    acc = jnp.int32(0)
    for g in range(N_EXPERT_GROUPS):
        bases.append(acc)
        acc = acc + ((counts[g] + (SEG_ALIGN - 1)) // SEG_ALIGN) * SEG_ALIGN
    pos_f = functools.reduce(lambda a, b: a + b,
                             [onehot[g:g + 1, :] * (rank[g:g + 1, :] + bases[g].astype(F32))
                              for g in range(N_EXPERT_GROUPS)])
    pos = pos_f.astype(jnp.int32)

    sort_mat = (lax.broadcasted_iota(jnp.int32, (SORT_SLOTS, T), 0) == pos).astype(BF16)
    hs_ref[:SORT_SLOTS, :] = _dot(sort_mat, h).astype(BF16)

    tok_t = jnp.concatenate(comb_rows + [pos_f, jnp.zeros((LANES - N_EXPERTS - 1, T), F32)], axis=0)
    tok = jnp.transpose(tok_t)
    hi = tok.astype(BF16)
    r1 = tok - hi.astype(F32)
    mid = r1.astype(BF16)
    lo = (r1 - mid.astype(F32)).astype(BF16)
    parts = _dot(sort_mat, jnp.concatenate([hi, mid, lo], axis=1))
    cs_ref[:SORT_SLOTS, :] = parts[:, :LANES] + parts[:, LANES:2 * LANES] + parts[:, 2 * LANES:]

    ys_ref[T:, :] = jnp.zeros((SORT_ROWS - T, D_MODEL), F32)

    for g in range(N_EXPERT_GROUPS):
        def body(k, carry, g=g):
            start = pl.multiple_of(bases[g] + k * SUB_TILE, SEG_ALIGN)
            hs = hs_ref[pl.ds(start, SUB_TILE), :]
            cw = cs_ref[pl.ds(start, SUB_TILE), :]
            gate = _dot(hs, wg_ref[g])
            up = _dot(hs, wu_ref[g])
            scale = jnp.concatenate(
                [jnp.broadcast_to(cw[:, g * EXPERTS_PER_GROUP + i:g * EXPERTS_PER_GROUP + i + 1],
                                  (SUB_TILE, D_FF_EXPERT)) for i in range(EXPERTS_PER_GROUP)], axis=1)
            hid = (_silu(gate) * up * scale).astype(BF16)
            ys_ref[pl.ds(start, SUB_TILE), :] = _dot(hid, wd_ref[g])
            return carry
        lax.fori_loop(0, (counts[g] + (SUB_TILE - 1)) // SUB_TILE, body, 0)

    unsort_mat = (tok[:, POS_LANE:POS_LANE + 1].astype(jnp.int32)
                  == lax.broadcasted_iota(jnp.int32, (T, UNSORT_COLS), 1)).astype(BF16)
    y = _dot(unsort_mat, ys_ref[:UNSORT_COLS, :].astype(BF16))
    z = DEEPNORM_ALPHA * x + (1.0 + g2) * y
    o_ref[...] = _layer_norm(z) * ln2_ref[0:1, :] + ln2_ref[1:2, :]


def _moe(x, mod_l, wr_t, rbias, wg, wu, wd, ln2):
    B, S, D = x.shape
    return pl.pallas_call(
        _moe_kernel,
        grid=(B, S // TOKEN_TILE),
        in_specs=[
            pl.BlockSpec((None, TOKEN_TILE, D), lambda b, j: (b, j, 0)),
            pl.BlockSpec((None, 6, D), lambda b, j: (b, 0, 0)),
            _resident(wr_t.shape),
            _resident(rbias.shape),
            _resident(wg.shape),
            _resident(wu.shape),
            _resident(wd.shape),
            _resident(ln2.shape),
        ],
        out_specs=pl.BlockSpec((None, TOKEN_TILE, D), lambda b, j: (b, j, 0)),
        out_shape=jax.ShapeDtypeStruct((B, S, D), F32),
        scratch_shapes=[pltpu.VMEM((SORT_ROWS, D_MODEL), BF16),
                        pltpu.VMEM((SORT_ROWS, LANES), F32),
                        pltpu.VMEM((SORT_ROWS, D_MODEL), F32)],
        compiler_params=pltpu.CompilerParams(
            dimension_semantics=("arbitrary", "arbitrary"),
            vmem_limit_bytes=VMEM_LIMIT_BYTES),
        name="grouped_moe",
    )(x, mod_l, wr_t, rbias, wg, wu, wd, ln2)


def _group_cols(w):
    w = w.reshape(N_EXPERT_GROUPS, EXPERTS_PER_GROUP, D_MODEL, D_FF_EXPERT)
    w = jnp.transpose(w, (0, 2, 1, 3))
    return w.reshape(N_EXPERT_GROUPS, D_MODEL, EXPERTS_PER_GROUP * D_FF_EXPERT).astype(BF16)


def kernel(x, c, rel_bias_table, w_router, router_bias, w_ada, b_ada, w_in, sinks, sg_w, sg_b, sg_ln_g, sg_ln_b,
           w_br_attn, w_br_sg, w_out, ln1_g, ln1_b, w_gate, w_up, w_down, ln2_g, ln2_b):
    L = w_in.shape[0]
    B = x.shape[0]
    mod = _ada_modulation(c, w_ada, b_ada).reshape(L, B, 6, D_MODEL)
    bias = _band_bias(rel_bias_table)
    bias = bias.reshape(N_KV_HEADS, Q_PER_KV * BLOCK, 2 * BLOCK)

    wr_t = jnp.transpose(w_router).astype(BF16)
    rbias = router_bias.reshape(N_EXPERTS, 1)
    for l in range(L):
        sink_col = jnp.repeat(sinks[l].reshape(N_KV_HEADS, Q_PER_KV), BLOCK, axis=1).reshape(
            N_KV_HEADS, Q_PER_KV * BLOCK, 1)
        sgw_pair = jnp.concatenate([sg_w[l, 0::2], sg_w[l, 1::2]], axis=-1).astype(BF16)
        sgb_full = jnp.repeat(jnp.transpose(sg_b[l]), SG_GROUP_DIM, axis=1)
        sgln = jnp.stack([sg_ln_g[l], sg_ln_b[l]])
        ln1 = jnp.stack([ln1_g[l], ln1_b[l]])
        ln2 = jnp.stack([ln2_g[l], ln2_b[l]])
        x = _token_mix(x, mod[l], w_in[l].astype(BF16), w_br_attn[l].astype(BF16), w_br_sg[l].astype(BF16),
                       w_out[l].astype(BF16), bias, sink_col, sgw_pair, sgb_full, sgln, ln1)
        x = _moe(x, mod[l], wr_t, rbias, _group_cols(w_gate[l]), _group_cols(w_up[l]),
                 w_down[l].reshape(N_EXPERT_GROUPS, EXPERTS_PER_GROUP * D_FF_EXPERT, D_MODEL).astype(BF16), ln2)
    return x
```

```python
import functools
import math

import numpy as np
import jax
import jax.numpy as jnp
from jax import lax
from jax.experimental import pallas as pl
from jax.experimental.pallas import tpu as pltpu

D_MODEL = 1024
DEPTH = 4
N_Q_HEADS = 8
N_KV_HEADS = 2
HEAD_DIM = 64
Q_PER_KV = N_Q_HEADS // N_KV_HEADS
WINDOW = 128
BLOCK = WINDOW
ATTN_WIDTH = N_Q_HEADS * HEAD_DIM
KV_WIDTH = N_KV_HEADS * HEAD_DIM
N_BUCKETS = 32
MAX_DISTANCE = 128
N_SG_GROUPS = 8
SG_GROUP_DIM = 64
SG_CHUNK = 128
SG_WIDTH = N_SG_GROUPS * SG_GROUP_DIM
IN_WIDTH = ATTN_WIDTH + 2 * KV_WIDTH + 2 * SG_WIDTH + 2 * D_MODEL
N_EXPERTS = 16
N_EXPERT_GROUPS = 4
EXPERTS_PER_GROUP = N_EXPERTS // N_EXPERT_GROUPS
TOP_K = 2
D_FF_EXPERT = 256
DEEPNORM_ALPHA = (2 * DEPTH) ** 0.25
LN_EPS = 1e-5
NEG_INF = -1e30

OFF_Q = 0
OFF_KV = ATTN_WIDTH
OFF_SU = ATTN_WIDTH + 2 * KV_WIDTH
OFF_SV = OFF_SU + SG_WIDTH
OFF_GA = OFF_SV + SG_WIDTH
OFF_GS = OFF_GA + D_MODEL

LANES = 128
TOKEN_TILE = 512
BLOCKS_PER_TILE = TOKEN_TILE // BLOCK
REST_SLAB = 256
VMEM_LIMIT_BYTES = 56 * 1024 * 1024

BF16 = jnp.bfloat16
F32 = jnp.float32


def _layer_norm(x):
    mu = jnp.mean(x, axis=-1, keepdims=True)
    xc = x - mu
    var = jnp.mean(xc * xc, axis=-1, keepdims=True)
    return xc * lax.rsqrt(var + LN_EPS)


def _dot(a, b):
    return jnp.dot(a, b, preferred_element_type=F32)


def _dot_nt(a, b):
    return lax.dot_general(a, b, (((1,), (1,)), ((), ())), preferred_element_type=F32)


def _gelu(x):
    return 0.5 * x * (1.0 + lax.erf(x * (1.0 / math.sqrt(2.0))))


def _sigmoid(x):
    return 0.5 * (jnp.tanh(0.5 * x) + 1.0)


def _silu(x):
    return x * _sigmoid(x)


ADA_TILE = 1536


def _ada_kernel(c_ref, w_ref, b_ref, o_ref):
    cond = _silu(c_ref[...]).astype(BF16)
    o_ref[...] = _dot(cond, w_ref[...].astype(BF16)) + b_ref[...]


def _ada_modulation(c, w_ada, b_ada):
    L, D, N = w_ada.shape
    B = c.shape[0]
    return pl.pallas_call(
        _ada_kernel,
        grid=(L, N // ADA_TILE),
        in_specs=[
            pl.BlockSpec((B, D), lambda l, n: (0, 0)),
            pl.BlockSpec((None, D, ADA_TILE), lambda l, n: (l, 0, n)),
            pl.BlockSpec((None, 1, ADA_TILE), lambda l, n: (l, 0, n)),
        ],
        out_specs=pl.BlockSpec((None, B, ADA_TILE), lambda l, n: (l, 0, n)),
        out_shape=jax.ShapeDtypeStruct((L, B, N), F32),
        compiler_params=pltpu.CompilerParams(
            dimension_semantics=("arbitrary", "arbitrary"),
            vmem_limit_bytes=VMEM_LIMIT_BYTES),
        name="ada_modulation",
    )(c, w_ada, b_ada.reshape(L, 1, N))


def _band_buckets():
    qi = np.arange(BLOCK)[:, None]
    kj = np.arange(2 * BLOCK)[None, :]
    rel = qi + BLOCK - kj
    n = np.maximum(rel, 0)
    max_exact = N_BUCKETS // 2
    nf = np.maximum(n, 1).astype(np.float32)
    large = max_exact + (np.log(nf / np.float32(max_exact)) / np.float32(math.log(MAX_DISTANCE / max_exact))
                         * np.float32(N_BUCKETS - max_exact)).astype(np.int32)
    large = np.minimum(large, N_BUCKETS - 1)
    bucket = np.where(n < max_exact, n, large).astype(np.int32)
    in_window = ((rel >= 0) & (rel < WINDOW)).astype(np.int32)
    return bucket, in_window


def _bias_kernel(table_ref, bucket_ref, window_ref, o_ref):
    bucket = bucket_ref[...]
    in_window = window_ref[...] > 0
    for head in range(N_Q_HEADS):
        acc = jnp.zeros((BLOCK, 2 * BLOCK), F32)
        for b in range(N_BUCKETS):
            acc = jnp.where(bucket == b, table_ref[b, head], acc)
        o_ref[head] = jnp.where(in_window, acc, NEG_INF)


def _band_bias(rel_bias_table):
    bucket, in_window = _band_buckets()
    return pl.pallas_call(
        _bias_kernel,
        in_specs=[
            pl.BlockSpec(memory_space=pltpu.SMEM),
            pl.BlockSpec(memory_space=pltpu.VMEM),
            pl.BlockSpec(memory_space=pltpu.VMEM),
        ],
        out_specs=pl.BlockSpec(memory_space=pltpu.VMEM),
        out_shape=jax.ShapeDtypeStruct((N_Q_HEADS, BLOCK, 2 * BLOCK), F32),
        name="band_bias",
    )(rel_bias_table, jnp.asarray(bucket), jnp.asarray(in_window))


def _token_mix_kernel(x_ref, mod_ref, w_in_ref, w_bra_ref, w_brs_ref, w_out_ref, bias_ref, sink_ref,
                      sgw_ref, sgb_ref, sgln_ref, ln1_ref, o_ref, kprev_ref, vprev_ref):
    j = pl.program_id(1)

    @pl.when(j == 0)
    def _():
        kprev_ref[...] = jnp.zeros_like(kprev_ref)
        vprev_ref[...] = jnp.zeros_like(vprev_ref)

    x = x_ref[...]
    sh1 = mod_ref[0:1, :]
    sc1 = mod_ref[1:2, :]
    g1 = mod_ref[2:3, :]
    h = (_layer_norm(x) * (1.0 + sc1) + sh1).astype(BF16)

    lane = lax.broadcasted_iota(jnp.int32, (1, LANES), 1)
    lo = lane < HEAD_DIM

    q = (_dot(h, w_in_ref[:, OFF_Q:OFF_Q + ATTN_WIDTH]) * (HEAD_DIM ** -0.5)).astype(BF16)
    kv = _dot(h, w_in_ref[:, OFF_KV:OFF_KV + 2 * KV_WIDTH]).astype(BF16)
    k_full = jnp.concatenate([kprev_ref[...], kv[:, :KV_WIDTH]], axis=0)
    v_full = jnp.concatenate([vprev_ref[...], kv[:, KV_WIDTH:]], axis=0)
    kprev_ref[...] = kv[TOKEN_TILE - BLOCK:, :KV_WIDTH]
    vprev_ref[...] = kv[TOKEN_TILE - BLOCK:, KV_WIDTH:]

    k_sw = pltpu.roll(k_full, HEAD_DIM, 1)
    k_dup = (jnp.where(lo, k_full, k_sw), jnp.where(lo, k_sw, k_full))
    one = jnp.ones((), BF16)
    v_ext = (jnp.where(lo, v_full, one), jnp.where(lo, one, v_full))

    n_rest = (IN_WIDTH - OFF_SU) // REST_SLAB
    rest = []

    def project_rest(count):
        for _ in range(count):
            c0 = OFF_SU + len(rest) * REST_SLAB
            rest.append(_dot(h, w_in_ref[:, c0:c0 + REST_SLAB]))

    key_in_prev = lax.broadcasted_iota(jnp.int32, (1, 2 * BLOCK), 1) < BLOCK
    zero = jnp.zeros((), BF16)
    units = [(n, hk) for n in range(BLOCKS_PER_TILE) for hk in range(N_KV_HEADS)]

    def scores(n, hk):
        r0 = n * BLOCK
        qa = q[r0:r0 + BLOCK, hk * 256:hk * 256 + LANES]
        qb = q[r0:r0 + BLOCK, hk * 256 + LANES:hk * 256 + 2 * LANES]
        qs = jnp.concatenate([jnp.where(lo, qa, zero), jnp.where(lo, zero, qa),
                              jnp.where(lo, qb, zero), jnp.where(lo, zero, qb)], axis=0)
        s = _dot_nt(qs, k_dup[hk][r0:r0 + 2 * BLOCK]) + bias_ref[hk]
        if n == 0:
            s = jnp.where(jnp.logical_and(j == 0, key_in_prev), NEG_INF, s)
        return s

    def attend(s, n, hk):
        r0 = n * BLOCK
        sink = sink_ref[hk]
        m = jnp.maximum(jnp.max(s, axis=-1, keepdims=True), sink)
        p = jnp.exp(s - m).astype(BF16)
        pv = _dot(p, v_ext[hk][r0:r0 + 2 * BLOCK])
        pv_sw = pltpu.roll(pv, HEAD_DIM, 1)
        row_sum = jnp.where(lo, pv_sw, pv) if hk == 0 else jnp.where(lo, pv, pv_sw)
        out = pv / (row_sum + jnp.exp(sink - m))
        out_sw = pltpu.roll(out, HEAD_DIM, 1)
        slabs = []
        for pair in range(2):
            first = slice(2 * pair * BLOCK, (2 * pair + 1) * BLOCK)
            second = slice((2 * pair + 1) * BLOCK, (2 * pair + 2) * BLOCK)
            if hk == 0:
                slabs.append(jnp.where(lo, out[first], out_sw[second]))
            else:
                slabs.append(jnp.where(lo, out_sw[first], out[second]))
        return slabs

    slabs = {}
    s_next = scores(*units[0])
    for i, (n, hk) in enumerate(units):
        s_cur = s_next
        project_rest(1)
        if i + 1 < len(units):
            s_next = scores(*units[i + 1])
        if i % 2 == 1:
            project_rest(1)
        slabs[(n, hk)] = attend(s_cur, n, hk)
    attn = jnp.concatenate(
        [jnp.concatenate(slabs[(n, 0)] + slabs[(n, 1)], axis=1).astype(BF16) for n in range(BLOCKS_PER_TILE)],
        axis=0)
    a_br = _dot(attn, w_bra_ref[...])
    project_rest(n_rest - len(rest))
    rest = jnp.concatenate(rest, axis=1)
    su = rest[:, :SG_WIDTH]
    sv = rest[:, SG_WIDTH:2 * SG_WIDTH]
    ga = rest[:, 2 * SG_WIDTH:2 * SG_WIDTH + D_MODEL]
    gs = rest[:, 2 * SG_WIDTH + D_MODEL:]

    u = _gelu(su)
    vn = (_layer_norm(_gelu(sv)) * sgln_ref[0:1, :] + sgln_ref[1:2, :]).astype(BF16)
    row = lax.broadcasted_iota(jnp.int32, (SG_CHUNK, 2 * SG_CHUNK), 0)
    col = lax.broadcasted_iota(jnp.int32, (SG_CHUNK, 2 * SG_CHUNK), 1)
    causal = (col & (SG_CHUNK - 1)) <= row
    mixed_cols = []
    for s_idx in range(SG_WIDTH // LANES):
        w_pair = jnp.where(causal, sgw_ref[s_idx], zero)
        b_slab = sgb_ref[:, s_idx * LANES:(s_idx + 1) * LANES]
        rows = []
        for c in range(TOKEN_TILE // SG_CHUNK):
            slab = vn[c * SG_CHUNK:(c + 1) * SG_CHUNK, s_idx * LANES:(s_idx + 1) * LANES]
            rhs = jnp.concatenate([jnp.where(lo, slab, zero), jnp.where(lo, zero, slab)], axis=0)
            rows.append(_dot(w_pair, rhs) + b_slab)
        mixed_cols.append(jnp.concatenate(rows, axis=0))
    mixed = jnp.concatenate(mixed_cols, axis=1)
    sg = (u * mixed).astype(BF16)
    s_br = _dot(sg, w_brs_ref[...])

    merged = (_sigmoid(ga) * a_br + _sigmoid(gs) * s_br).astype(BF16)
    y = _dot(merged, w_out_ref[...])
    z = DEEPNORM_ALPHA * x + (1.0 + g1) * y
    o_ref[...] = _layer_norm(z) * ln1_ref[0:1, :] + ln1_ref[1:2, :]


def _resident(shape):
    ndim = len(shape)
    return pl.BlockSpec(shape, lambda b, j: (0,) * ndim, pipeline_mode=pl.Buffered(1))


def _resident_layer(stacked, layer):
    shape = stacked.shape[1:]
    ndim = len(shape)
    return pl.BlockSpec((None,) + shape, lambda b, j: (layer,) + (0,) * ndim, pipeline_mode=pl.Buffered(1))


def _mod_spec(layer):
    return pl.BlockSpec((None, None, 6, D_MODEL), lambda b, j: (layer, b, 0, 0))


def _token_mix(layer, x, mod, w_in, w_bra, w_brs, w_out, bias, sink_col, sgw_pair, sgb_full, sgln, ln1):
    B, S, D = x.shape
    return pl.pallas_call(
        _token_mix_kernel,
        grid=(B, S // TOKEN_TILE),
        in_specs=[
            pl.BlockSpec((None, TOKEN_TILE, D), lambda b, j: (b, j, 0)),
            _mod_spec(layer),
            _resident_layer(w_in, layer),
            _resident_layer(w_bra, layer),
            _resident_layer(w_brs, layer),
            _resident_layer(w_out, layer),
            _resident(bias.shape),
            _resident_layer(sink_col, layer),
            _resident_layer(sgw_pair, layer),
            _resident_layer(sgb_full, layer),
            _resident_layer(sgln, layer),
            _resident_layer(ln1, layer),
        ],
        out_specs=pl.BlockSpec((None, TOKEN_TILE, D), lambda b, j: (b, j, 0)),
        out_shape=jax.ShapeDtypeStruct((B, S, D), F32),
        scratch_shapes=[pltpu.VMEM((BLOCK, KV_WIDTH), BF16), pltpu.VMEM((BLOCK, KV_WIDTH), BF16)],
        compiler_params=pltpu.CompilerParams(
            dimension_semantics=("arbitrary", "arbitrary"),
            vmem_limit_bytes=VMEM_LIMIT_BYTES),
        name="token_mix",
    )(x, mod, w_in, w_bra, w_brs, w_out, bias, sink_col, sgw_pair, sgb_full, sgln, ln1)


def _route(logits_t, rbias):
    rows = [logits_t[e:e + 1, :] for e in range(N_EXPERTS)]
    m = functools.reduce(jnp.maximum, rows)
    ex = [jnp.exp(r - m) for r in rows]
    denom = functools.reduce(lambda a, b: a + b, ex)
    probs = [e_ / denom for e_ in ex]
    sel = [probs[e] + rbias[e:e + 1, :] for e in range(N_EXPERTS)]

    def beats(a, b, a_first):
        return (a >= b) if a_first else (a > b)

    chosen = []
    score = []
    for g in range(N_EXPERT_GROUPS):
        members = list(range(g * EXPERTS_PER_GROUP, (g + 1) * EXPERTS_PER_GROUP))
        total = None
        for i in members:
            rank = None
            for o in members:
                if o == i:
                    continue
                ahead = beats(sel[o], sel[i], o < i).astype(F32)
                rank = ahead if rank is None else rank + ahead
            pick = rank < TOP_K
            chosen.append(pick)
            term = jnp.where(pick, sel[i], 0.0)
            total = term if total is None else total + term
        score.append(total)
    best = []
    for g in range(N_EXPERT_GROUPS):
        is_best = None
        for o in range(N_EXPERT_GROUPS):
            if o == g:
                continue
            wins = beats(score[g], score[o], g < o)
            is_best = wins if is_best is None else jnp.logical_and(is_best, wins)
        best.append(is_best)
    picked = [jnp.where(jnp.logical_and(chosen[e], best[e // EXPERTS_PER_GROUP]), probs[e], 0.0)
              for e in range(N_EXPERTS)]
    norm = functools.reduce(lambda a, b: a + b, picked)
    return [p_ / norm for p_ in picked], best


SUB_TILE = 144
SEG_ALIGN = 16
SORT_SLOTS = TOKEN_TILE + N_EXPERT_GROUPS * SEG_ALIGN
SORT_ROWS = SORT_SLOTS + SUB_TILE
UNSORT_COLS = -(-SORT_SLOTS // LANES) * LANES
POS_LANE = N_EXPERTS


def _moe_kernel(x_ref, mod_ref, wr_ref, rb_ref, before_ref, wg_ref, wu_ref, wd_ref, ln2_ref, o_ref,
                hs_ref, cs_ref, ys_ref):
    T = TOKEN_TILE

    @pl.when(jnp.logical_and(pl.program_id(0) == 0, pl.program_id(1) == 0))
    def _():
        hs_ref[SORT_SLOTS:, :] = jnp.zeros((SORT_ROWS - SORT_SLOTS, D_MODEL), BF16)
        cs_ref[SORT_SLOTS:, :] = jnp.zeros((SORT_ROWS - SORT_SLOTS, LANES), F32)

    x = x_ref[...]
    sh2 = mod_ref[3:4, :]
    sc2 = mod_ref[4:5, :]
    g2 = mod_ref[5:6, :]
    h = (_layer_norm(x) * (1.0 + sc2) + sh2).astype(BF16)

    logits_t = _dot_nt(wr_ref[...], h)
    comb_rows, best = _route(logits_t, rb_ref[...])

    onehot = jnp.concatenate([b.astype(F32) for b in best] + [jnp.zeros((8 - N_EXPERT_GROUPS, T), F32)], axis=0)
    rank = _dot(onehot.astype(BF16), before_ref[...])
    counts = [jnp.sum(onehot[g:g + 1, :]).astype(jnp.int32) for g in range(N_EXPERT_GROUPS)]
    bases = []
    acc = jnp.int32(0)
    for g in range(N_EXPERT_GROUPS):
        bases.append(acc)
        acc = acc + ((counts[g] + (SEG_ALIGN - 1)) // SEG_ALIGN) * SEG_ALIGN
    pos_f = functools.reduce(lambda a, b: a + b,
                             [onehot[g:g + 1, :] * (rank[g:g + 1, :] + bases[g].astype(F32))
                              for g in range(N_EXPERT_GROUPS)])
    pos = pos_f.astype(jnp.int32)

    sort_mat = (lax.broadcasted_iota(jnp.int32, (SORT_SLOTS, T), 0) == pos).astype(BF16)
    hs_ref[:SORT_SLOTS, :] = _dot(sort_mat, h).astype(BF16)

    tok_t = jnp.concatenate(comb_rows + [pos_f, jnp.zeros((LANES - N_EXPERTS - 1, T), F32)], axis=0)
    tok = jnp.transpose(tok_t)
    hi = tok.astype(BF16)
    r1 = tok - hi.astype(F32)
    mid = r1.astype(BF16)
    lo = (r1 - mid.astype(F32)).astype(BF16)
    parts = _dot(sort_mat, jnp.concatenate([hi, mid, lo], axis=1))
    cs_ref[:SORT_SLOTS, :] = parts[:, :LANES] + parts[:, LANES:2 * LANES] + parts[:, 2 * LANES:]

    ys_ref[T:UNSORT_COLS, :] = jnp.zeros((UNSORT_COLS - T, D_MODEL), F32)

    for g in range(N_EXPERT_GROUPS):
        def body(k, carry, g=g):
            start = pl.multiple_of(bases[g] + k * SUB_TILE, SEG_ALIGN)
            hs = hs_ref[pl.ds(start, SUB_TILE), :]
            cw = cs_ref[pl.ds(start, SUB_TILE), :]
            experts = range(g * EXPERTS_PER_GROUP, (g + 1) * EXPERTS_PER_GROUP)
            gate = jnp.concatenate([_dot(hs, wg_ref[e]) for e in experts], axis=1)
            up = jnp.concatenate([_dot(hs, wu_ref[e]) for e in experts], axis=1)
            scale = jnp.concatenate(
                [jnp.broadcast_to(cw[:, e:e + 1], (SUB_TILE, D_FF_EXPERT)) for e in experts], axis=1)
            hid = (_silu(gate) * up * scale).astype(BF16)
            ys_ref[pl.ds(start, SUB_TILE), :] = _dot(hid, wd_ref[g])
            return carry
        lax.fori_loop(0, (counts[g] + (SUB_TILE - 1)) // SUB_TILE, body, 0)

    unsort_mat = (tok[:, POS_LANE:POS_LANE + 1].astype(jnp.int32)
                  == lax.broadcasted_iota(jnp.int32, (T, UNSORT_COLS), 1)).astype(BF16)
    y = _dot(unsort_mat, ys_ref[:UNSORT_COLS, :].astype(BF16))
    z = DEEPNORM_ALPHA * x + (1.0 + g2) * y
    o_ref[...] = _layer_norm(z) * ln2_ref[0:1, :] + ln2_ref[1:2, :]


def _moe(layer, x, mod, wr_t, rbias, wg, wu, wd, ln2):
    B, S, D = x.shape
    before = jnp.asarray(np.triu(np.ones((TOKEN_TILE, TOKEN_TILE), np.float32), k=1), dtype=BF16)
    return pl.pallas_call(
        _moe_kernel,
        grid=(B, S // TOKEN_TILE),
        in_specs=[
            pl.BlockSpec((None, TOKEN_TILE, D), lambda b, j: (b, j, 0)),
            _mod_spec(layer),
            _resident(wr_t.shape),
            _resident(rbias.shape),
            _resident(before.shape),
            _resident_layer(wg, layer),
            _resident_layer(wu, layer),
            _resident_layer(wd, layer),
            _resident_layer(ln2, layer),
        ],
        out_specs=pl.BlockSpec((None, TOKEN_TILE, D), lambda b, j: (b, j, 0)),
        out_shape=jax.ShapeDtypeStruct((B, S, D), F32),
        scratch_shapes=[pltpu.VMEM((SORT_ROWS, D_MODEL), BF16),
                        pltpu.VMEM((SORT_ROWS, LANES), F32),
                        pltpu.VMEM((SORT_ROWS, D_MODEL), F32)],
        compiler_params=pltpu.CompilerParams(
            dimension_semantics=("arbitrary", "arbitrary"),
            vmem_limit_bytes=VMEM_LIMIT_BYTES),
        name="grouped_moe",
    )(x, mod, wr_t, rbias, before, wg, wu, wd, ln2)


def kernel(x, c, rel_bias_table, w_router, router_bias, w_ada, b_ada, w_in, sinks, sg_w, sg_b, sg_ln_g, sg_ln_b,
           w_br_attn, w_br_sg, w_out, ln1_g, ln1_b, w_gate, w_up, w_down, ln2_g, ln2_b):
    L = w_in.shape[0]
    B = x.shape[0]
    mod = _ada_modulation(c, w_ada, b_ada).reshape(L, B, 6, D_MODEL)
    bias = _band_bias(rel_bias_table)
    bias = bias.reshape(N_KV_HEADS, Q_PER_KV * BLOCK, 2 * BLOCK)

    w_in_b = w_in.astype(BF16)
    w_bra_b = w_br_attn.astype(BF16)
    w_brs_b = w_br_sg.astype(BF16)
    w_out_b = w_out.astype(BF16)
    wg_b = w_gate.astype(BF16)
    wu_b = w_up.astype(BF16)
    wd_b = w_down.reshape(L, N_EXPERT_GROUPS, EXPERTS_PER_GROUP * D_FF_EXPERT, D_MODEL).astype(BF16)
    wr_t = jnp.transpose(w_router).astype(BF16)
    rbias = router_bias.reshape(N_EXPERTS, 1)
    sink_col = jnp.repeat(sinks.reshape(L, N_KV_HEADS, Q_PER_KV), BLOCK, axis=2).reshape(
        L, N_KV_HEADS, Q_PER_KV * BLOCK, 1)
    sgw_pair = jnp.concatenate([sg_w[:, 0::2], sg_w[:, 1::2]], axis=-1).astype(BF16)
    sgb_full = jnp.repeat(jnp.swapaxes(sg_b, 1, 2), SG_GROUP_DIM, axis=2)
    sgln = jnp.stack([sg_ln_g, sg_ln_b], axis=1)
    ln1 = jnp.stack([ln1_g, ln1_b], axis=1)
    ln2 = jnp.stack([ln2_g, ln2_b], axis=1)
    for l in range(L):
        x = _token_mix(l, x, mod, w_in_b, w_bra_b, w_brs_b, w_out_b, bias, sink_col, sgw_pair, sgb_full, sgln, ln1)
        x = _moe(l, x, mod, wr_t, rbias, wg_b, wu_b, wd_b, ln2)
    return x
```

```python
import functools
import math

import numpy as np
import jax
import jax.numpy as jnp
from jax import lax
from jax.experimental import pallas as pl
from jax.experimental.pallas import tpu as pltpu

D_MODEL = 1024
DEPTH = 4
N_Q_HEADS = 8
N_KV_HEADS = 2
HEAD_DIM = 64
Q_PER_KV = N_Q_HEADS // N_KV_HEADS
WINDOW = 128
BLOCK = WINDOW
ATTN_WIDTH = N_Q_HEADS * HEAD_DIM
KV_WIDTH = N_KV_HEADS * HEAD_DIM
N_BUCKETS = 32
MAX_DISTANCE = 128
N_SG_GROUPS = 8
SG_GROUP_DIM = 64
SG_CHUNK = 128
SG_WIDTH = N_SG_GROUPS * SG_GROUP_DIM
IN_WIDTH = ATTN_WIDTH + 2 * KV_WIDTH + 2 * SG_WIDTH + 2 * D_MODEL
N_EXPERTS = 16
N_EXPERT_GROUPS = 4
EXPERTS_PER_GROUP = N_EXPERTS // N_EXPERT_GROUPS
TOP_K = 2
D_FF_EXPERT = 256
DEEPNORM_ALPHA = (2 * DEPTH) ** 0.25
LN_EPS = 1e-5
NEG_INF = -1e30

OFF_Q = 0
OFF_KV = ATTN_WIDTH
OFF_SU = ATTN_WIDTH + 2 * KV_WIDTH
OFF_SV = OFF_SU + SG_WIDTH
OFF_GA = OFF_SV + SG_WIDTH
OFF_GS = OFF_GA + D_MODEL

LANES = 128
TOKEN_TILE = 512
BLOCKS_PER_TILE = TOKEN_TILE // BLOCK
REST_SLAB = 256
VMEM_LIMIT_BYTES = 56 * 1024 * 1024

BF16 = jnp.bfloat16
F32 = jnp.float32


def _layer_norm(x):
    mu = jnp.mean(x, axis=-1, keepdims=True)
    xc = x - mu
    var = jnp.mean(xc * xc, axis=-1, keepdims=True)
    return xc * lax.rsqrt(var + LN_EPS)


def _dot(a, b):
    return jnp.dot(a, b, preferred_element_type=F32)


def _dot_nt(a, b):
    return lax.dot_general(a, b, (((1,), (1,)), ((), ())), preferred_element_type=F32)


def _gelu(x):
    return 0.5 * x * (1.0 + lax.erf(x * (1.0 / math.sqrt(2.0))))


def _sigmoid(x):
    return 0.5 * (jnp.tanh(0.5 * x) + 1.0)


def _silu(x):
    return x * _sigmoid(x)


ADA_TILE = 1536


def _ada_kernel(c_ref, w_ref, b_ref, o_ref):
    cond = _silu(c_ref[...]).astype(BF16)
    o_ref[...] = _dot(cond, w_ref[...].astype(BF16)) + b_ref[...]


def _ada_modulation(c, w_ada, b_ada):
    L, D, N = w_ada.shape
    B = c.shape[0]
    return pl.pallas_call(
        _ada_kernel,
        grid=(L, N // ADA_TILE),
        in_specs=[
            pl.BlockSpec((B, D), lambda l, n: (0, 0)),
            pl.BlockSpec((None, D, ADA_TILE), lambda l, n: (l, 0, n)),
            pl.BlockSpec((None, 1, ADA_TILE), lambda l, n: (l, 0, n)),
        ],
        out_specs=pl.BlockSpec((None, B, ADA_TILE), lambda l, n: (l, 0, n)),
        out_shape=jax.ShapeDtypeStruct((L, B, N), F32),
        compiler_params=pltpu.CompilerParams(
            dimension_semantics=("arbitrary", "arbitrary"),
            vmem_limit_bytes=VMEM_LIMIT_BYTES),
        name="ada_modulation",
    )(c, w_ada, b_ada.reshape(L, 1, N))


def _band_buckets():
    qi = np.arange(BLOCK)[:, None]
    kj = np.arange(2 * BLOCK)[None, :]
    rel = qi + BLOCK - kj
    n = np.maximum(rel, 0)
    max_exact = N_BUCKETS // 2
    nf = np.maximum(n, 1).astype(np.float32)
    large = max_exact + (np.log(nf / np.float32(max_exact)) / np.float32(math.log(MAX_DISTANCE / max_exact))
                         * np.float32(N_BUCKETS - max_exact)).astype(np.int32)
    large = np.minimum(large, N_BUCKETS - 1)
    bucket = np.where(n < max_exact, n, large).astype(np.int32)
    in_window = ((rel >= 0) & (rel < WINDOW)).astype(np.int32)
    return bucket, in_window


def _bias_kernel(table_ref, bucket_ref, window_ref, o_ref):
    bucket = bucket_ref[...]
    in_window = window_ref[...] > 0
    for head in range(N_Q_HEADS):
        acc = jnp.zeros((BLOCK, 2 * BLOCK), F32)
        for b in range(N_BUCKETS):
            acc = jnp.where(bucket == b, table_ref[b, head], acc)
        o_ref[head] = jnp.where(in_window, acc, NEG_INF)


def _band_bias(rel_bias_table):
    bucket, in_window = _band_buckets()
    return pl.pallas_call(
        _bias_kernel,
        in_specs=[
            pl.BlockSpec(memory_space=pltpu.SMEM),
            pl.BlockSpec(memory_space=pltpu.VMEM),
            pl.BlockSpec(memory_space=pltpu.VMEM),
        ],
        out_specs=pl.BlockSpec(memory_space=pltpu.VMEM),
        out_shape=jax.ShapeDtypeStruct((N_Q_HEADS, BLOCK, 2 * BLOCK), F32),
        name="band_bias",
    )(rel_bias_table, jnp.asarray(bucket), jnp.asarray(in_window))


def _token_mix_kernel(x_ref, mod_ref, w_in_ref, w_bra_ref, w_brs_ref, w_out_ref, bias_ref, sink_ref,
                      sgw_ref, sgb_ref, sgln_ref, ln1_ref, o_ref, kprev_ref, vprev_ref):
    j = pl.program_id(1)

    @pl.when(j == 0)
    def _():
        kprev_ref[...] = jnp.zeros_like(kprev_ref)
        vprev_ref[...] = jnp.zeros_like(vprev_ref)

    x = x_ref[...]
    sh1 = mod_ref[0:1, :]
    sc1 = mod_ref[1:2, :]
    g1 = mod_ref[2:3, :]
    h = (_layer_norm(x) * (1.0 + sc1) + sh1).astype(BF16)

    lane = lax.broadcasted_iota(jnp.int32, (1, LANES), 1)
    lo = lane < HEAD_DIM

    q = (_dot(h, w_in_ref[:, OFF_Q:OFF_Q + ATTN_WIDTH]) * (HEAD_DIM ** -0.5)).astype(BF16)
    kv = _dot(h, w_in_ref[:, OFF_KV:OFF_KV + 2 * KV_WIDTH]).astype(BF16)
    k_full = jnp.concatenate([kprev_ref[...], kv[:, :KV_WIDTH]], axis=0)
    v_full = jnp.concatenate([vprev_ref[...], kv[:, KV_WIDTH:]], axis=0)
    kprev_ref[...] = kv[TOKEN_TILE - BLOCK:, :KV_WIDTH]
    vprev_ref[...] = kv[TOKEN_TILE - BLOCK:, KV_WIDTH:]

    k_sw = pltpu.roll(k_full, HEAD_DIM, 1)
    k_dup = (jnp.where(lo, k_full, k_sw), jnp.where(lo, k_sw, k_full))
    one = jnp.ones((), BF16)
    v_ext = (jnp.where(lo, v_full, one), jnp.where(lo, one, v_full))

    n_rest = (IN_WIDTH - OFF_SU) // REST_SLAB
    rest = []

    def project_rest(count):
        for _ in range(count):
            c0 = OFF_SU + len(rest) * REST_SLAB
            rest.append(_dot(h, w_in_ref[:, c0:c0 + REST_SLAB]))

    key_in_prev = lax.broadcasted_iota(jnp.int32, (1, 2 * BLOCK), 1) < BLOCK
    zero = jnp.zeros((), BF16)
    units = [(n, hk) for n in range(BLOCKS_PER_TILE) for hk in range(N_KV_HEADS)]

    def scores(n, hk):
        r0 = n * BLOCK
        qa = q[r0:r0 + BLOCK, hk * 256:hk * 256 + LANES]
        qb = q[r0:r0 + BLOCK, hk * 256 + LANES:hk * 256 + 2 * LANES]
        qs = jnp.concatenate([jnp.where(lo, qa, zero), jnp.where(lo, zero, qa),
                              jnp.where(lo, qb, zero), jnp.where(lo, zero, qb)], axis=0)
        s = _dot_nt(qs, k_dup[hk][r0:r0 + 2 * BLOCK]) + bias_ref[hk]
        if n == 0:
            s = jnp.where(jnp.logical_and(j == 0, key_in_prev), NEG_INF, s)
        return s

    def attend(s, n, hk):
        r0 = n * BLOCK
        sink = sink_ref[hk]
        m = jnp.maximum(jnp.max(s, axis=-1, keepdims=True), sink)
        p = jnp.exp(s - m).astype(BF16)
        pv = _dot(p, v_ext[hk][r0:r0 + 2 * BLOCK])
        pv_sw = pltpu.roll(pv, HEAD_DIM, 1)
        row_sum = jnp.where(lo, pv_sw, pv) if hk == 0 else jnp.where(lo, pv, pv_sw)
        out = pv / (row_sum + jnp.exp(sink - m))
        out_sw = pltpu.roll(out, HEAD_DIM, 1)
        slabs = []
        for pair in range(2):
            first = slice(2 * pair * BLOCK, (2 * pair + 1) * BLOCK)
            second = slice((2 * pair + 1) * BLOCK, (2 * pair + 2) * BLOCK)
            if hk == 0:
                slabs.append(jnp.where(lo, out[first], out_sw[second]))
            else:
                slabs.append(jnp.where(lo, out_sw[first], out[second]))
        return slabs

    slabs = {}
    s_next = scores(*units[0])
    for i, (n, hk) in enumerate(units):
        s_cur = s_next
        project_rest(1)
        if i + 1 < len(units):
            s_next = scores(*units[i + 1])
        if i % 2 == 1:
            project_rest(1)
        slabs[(n, hk)] = attend(s_cur, n, hk)
    attn = jnp.concatenate(
        [jnp.concatenate(slabs[(n, 0)] + slabs[(n, 1)], axis=1).astype(BF16) for n in range(BLOCKS_PER_TILE)],
        axis=0)
    a_br = _dot(attn, w_bra_ref[...])
    project_rest(n_rest - len(rest))
    rest = jnp.concatenate(rest, axis=1)
    su = rest[:, :SG_WIDTH]
    sv = rest[:, SG_WIDTH:2 * SG_WIDTH]
    ga = rest[:, 2 * SG_WIDTH:2 * SG_WIDTH + D_MODEL]
    gs = rest[:, 2 * SG_WIDTH + D_MODEL:]

    u = _gelu(su)
    vn = (_layer_norm(_gelu(sv)) * sgln_ref[0:1, :] + sgln_ref[1:2, :]).astype(BF16)
    row = lax.broadcasted_iota(jnp.int32, (SG_CHUNK, 2 * SG_CHUNK), 0)
    col = lax.broadcasted_iota(jnp.int32, (SG_CHUNK, 2 * SG_CHUNK), 1)
    causal = (col & (SG_CHUNK - 1)) <= row
    mixed_cols = []
    for s_idx in range(SG_WIDTH // LANES):
        w_pair = jnp.where(causal, sgw_ref[s_idx], zero)
        b_slab = sgb_ref[:, s_idx * LANES:(s_idx + 1) * LANES]
        rows = []
        for c in range(TOKEN_TILE // SG_CHUNK):
            slab = vn[c * SG_CHUNK:(c + 1) * SG_CHUNK, s_idx * LANES:(s_idx + 1) * LANES]
            rhs = jnp.concatenate([jnp.where(lo, slab, zero), jnp.where(lo, zero, slab)], axis=0)
            rows.append(_dot(w_pair, rhs) + b_slab)
        mixed_cols.append(jnp.concatenate(rows, axis=0))
    mixed = jnp.concatenate(mixed_cols, axis=1)
    sg = (u * mixed).astype(BF16)
    s_br = _dot(sg, w_brs_ref[...])

    merged = (_sigmoid(ga) * a_br + _sigmoid(gs) * s_br).astype(BF16)
    y = _dot(merged, w_out_ref[...])
    z = DEEPNORM_ALPHA * x + (1.0 + g1) * y
    o_ref[...] = _layer_norm(z) * ln1_ref[0:1, :] + ln1_ref[1:2, :]


def _resident(shape):
    ndim = len(shape)
    return pl.BlockSpec(shape, lambda *_: (0,) * ndim, pipeline_mode=pl.Buffered(1))


def _resident_layer(stacked, layer):
    shape = stacked.shape[1:]
    ndim = len(shape)
    return pl.BlockSpec((None,) + shape, lambda *_: (layer,) + (0,) * ndim, pipeline_mode=pl.Buffered(1))


def _mod_spec(layer):
    return pl.BlockSpec((None, None, 6, D_MODEL), lambda b, j: (layer, b, 0, 0))


def _token_mix(layer, x, mod, w_in, w_bra, w_brs, w_out, bias, sink_col, sgw_pair, sgb_full, sgln, ln1):
    B, S, D = x.shape
    return pl.pallas_call(
        _token_mix_kernel,
        grid=(B, S // TOKEN_TILE),
        in_specs=[
            pl.BlockSpec((None, TOKEN_TILE, D), lambda b, j: (b, j, 0)),
            _mod_spec(layer),
            _resident_layer(w_in, layer),
            _resident_layer(w_bra, layer),
            _resident_layer(w_brs, layer),
            _resident_layer(w_out, layer),
            _resident(bias.shape),
            _resident_layer(sink_col, layer),
            _resident_layer(sgw_pair, layer),
            _resident_layer(sgb_full, layer),
            _resident_layer(sgln, layer),
            _resident_layer(ln1, layer),
        ],
        out_specs=pl.BlockSpec((None, TOKEN_TILE, D), lambda b, j: (b, j, 0)),
        out_shape=jax.ShapeDtypeStruct((B, S, D), F32),
        scratch_shapes=[pltpu.VMEM((BLOCK, KV_WIDTH), BF16), pltpu.VMEM((BLOCK, KV_WIDTH), BF16)],
        compiler_params=pltpu.CompilerParams(
            dimension_semantics=("arbitrary", "arbitrary"),
            vmem_limit_bytes=VMEM_LIMIT_BYTES),
        name="token_mix",
    )(x, mod, w_in, w_bra, w_brs, w_out, bias, sink_col, sgw_pair, sgb_full, sgln, ln1)


def _route(logits_t, rbias):
    rows = [logits_t[e:e + 1, :] for e in range(N_EXPERTS)]
    m = functools.reduce(jnp.maximum, rows)
    ex = [jnp.exp(r - m) for r in rows]
    denom = functools.reduce(lambda a, b: a + b, ex)
    probs = [e_ / denom for e_ in ex]
    sel = [probs[e] + rbias[e:e + 1, :] for e in range(N_EXPERTS)]

    def beats(a, b, a_first):
        return (a >= b) if a_first else (a > b)

    chosen = []
    score = []
    for g in range(N_EXPERT_GROUPS):
        members = list(range(g * EXPERTS_PER_GROUP, (g + 1) * EXPERTS_PER_GROUP))
        total = None
        for i in members:
            rank = None
            for o in members:
                if o == i:
                    continue
                ahead = beats(sel[o], sel[i], o < i).astype(F32)
                rank = ahead if rank is None else rank + ahead
            pick = rank < TOP_K
            chosen.append(pick)
            term = jnp.where(pick, sel[i], 0.0)
            total = term if total is None else total + term
        score.append(total)
    best = []
    for g in range(N_EXPERT_GROUPS):
        is_best = None
        for o in range(N_EXPERT_GROUPS):
            if o == g:
                continue
            wins = beats(score[g], score[o], g < o)
            is_best = wins if is_best is None else jnp.logical_and(is_best, wins)
        best.append(is_best)
    picked = [jnp.where(jnp.logical_and(chosen[e], best[e // EXPERTS_PER_GROUP]), probs[e], 0.0)
              for e in range(N_EXPERTS)]
    norm = functools.reduce(lambda a, b: a + b, picked)
    return [p_ / norm for p_ in picked], best


SUB_TILE = 144
SEG_ALIGN = 16
SORT_SLOTS = TOKEN_TILE + N_EXPERT_GROUPS * SEG_ALIGN
SORT_ROWS = SORT_SLOTS + SUB_TILE
UNSORT_COLS = -(-SORT_SLOTS // LANES) * LANES
POS_LANE = N_EXPERTS


def _expert_tile(hs_ref, cs_ref, wg_ref, wu_ref, wd_ref, g, start):
    hs = hs_ref[pl.ds(start, SUB_TILE), :]
    cw = cs_ref[pl.ds(start, SUB_TILE), :]
    experts = range(g * EXPERTS_PER_GROUP, (g + 1) * EXPERTS_PER_GROUP)
    gate = jnp.concatenate([_dot(hs, wg_ref[e]) for e in experts], axis=1)
    up = jnp.concatenate([_dot(hs, wu_ref[e]) for e in experts], axis=1)
    scale = jnp.concatenate(
        [jnp.broadcast_to(cw[:, e:e + 1], (SUB_TILE, D_FF_EXPERT)) for e in experts], axis=1)
    hid = (_silu(gate) * up * scale).astype(BF16)
    return _dot(hid, wd_ref[g])


def _moe_kernel(xc_ref, xn_ref, modc_ref, modn_ref, wr_ref, rb_ref, before_ref, wg_ref, wu_ref, wd_ref, ln2_ref,
                o_ref, h_ref, z_ref, hs_ref, cs_ref, ys_ref):
    T = TOKEN_TILE

    def modulated_norm(x_ref, mod_ref):
        return (_layer_norm(x_ref[...]) * (1.0 + mod_ref[4:5, :]) + mod_ref[3:4, :]).astype(BF16)

    @pl.when(pl.program_id(0) == 0)
    def _():
        hs_ref[SORT_SLOTS:, :] = jnp.zeros((SORT_ROWS - SORT_SLOTS, D_MODEL), BF16)
        cs_ref[SORT_SLOTS:, :] = jnp.zeros((SORT_ROWS - SORT_SLOTS, LANES), F32)
        ys_ref[...] = jnp.zeros_like(ys_ref)
        z_ref[...] = jnp.zeros_like(z_ref)
        h_ref[...] = modulated_norm(xc_ref, modc_ref)

    h = h_ref[...]
    logits_t = _dot_nt(wr_ref[...], h)
    comb_rows, best = _route(logits_t, rb_ref[...])

    onehot = jnp.concatenate([b.astype(F32) for b in best] + [jnp.zeros((8 - N_EXPERT_GROUPS, T), F32)], axis=0)
    rank = _dot(onehot.astype(BF16), before_ref[...])
    counts = [jnp.sum(onehot[g:g + 1, :]).astype(jnp.int32) for g in range(N_EXPERT_GROUPS)]
    bases = []
    acc = jnp.int32(0)
    for g in range(N_EXPERT_GROUPS):
        bases.append(acc)
        acc = acc + ((counts[g] + (SEG_ALIGN - 1)) // SEG_ALIGN) * SEG_ALIGN
    pos_f = functools.reduce(lambda a, b: a + b,
                             [onehot[g:g + 1, :] * (rank[g:g + 1, :] + bases[g].astype(F32))
                              for g in range(N_EXPERT_GROUPS)])
    pos = pos_f.astype(jnp.int32)

    sort_mat = (lax.broadcasted_iota(jnp.int32, (SORT_SLOTS, T), 0) == pos).astype(BF16)
    hs_ref[:SORT_SLOTS, :] = _dot(sort_mat, h).astype(BF16)

    tok_t = jnp.concatenate(comb_rows + [pos_f, jnp.zeros((LANES - N_EXPERTS - 1, T), F32)], axis=0)
    tok = jnp.transpose(tok_t)
    hi = tok.astype(BF16)
    r1 = tok - hi.astype(F32)
    mid = r1.astype(BF16)
    lo = (r1 - mid.astype(F32)).astype(BF16)
    parts = _dot(sort_mat, jnp.concatenate([hi, mid, lo], axis=1))
    cs_ref[:SORT_SLOTS, :] = parts[:, :LANES] + parts[:, LANES:2 * LANES] + parts[:, 2 * LANES:]

    ys_ref[T:UNSORT_COLS, :] = jnp.zeros((UNSORT_COLS - T, D_MODEL), F32)

    rows_per_group = T // N_EXPERT_GROUPS
    for g in range(N_EXPERT_GROUPS):
        rows = slice(g * rows_per_group, (g + 1) * rows_per_group)
        h_ref[rows, :] = (_layer_norm(xn_ref[rows, :]) * (1.0 + modn_ref[4:5, :]) + modn_ref[3:4, :]).astype(BF16)
        o_ref[rows, :] = _layer_norm(z_ref[rows, :]) * ln2_ref[0:1, :] + ln2_ref[1:2, :]
        start = pl.multiple_of(bases[g], SEG_ALIGN)
        ys_ref[pl.ds(start, SUB_TILE), :] = _expert_tile(hs_ref, cs_ref, wg_ref, wu_ref, wd_ref, g, start)

        def body(k, carry, g=g):
            start = pl.multiple_of(bases[g] + k * SUB_TILE, SEG_ALIGN)
            ys_ref[pl.ds(start, SUB_TILE), :] = _expert_tile(hs_ref, cs_ref, wg_ref, wu_ref, wd_ref, g, start)
            return carry
        lax.fori_loop(1, (counts[g] + (SUB_TILE - 1)) // SUB_TILE, body, 0)

    unsort_mat = (tok[:, POS_LANE:POS_LANE + 1].astype(jnp.int32)
                  == lax.broadcasted_iota(jnp.int32, (T, UNSORT_COLS), 1)).astype(BF16)
    y = _dot(unsort_mat, ys_ref[:UNSORT_COLS, :].astype(BF16))
    z_ref[...] = DEEPNORM_ALPHA * xc_ref[...] + (1.0 + modc_ref[5:6, :]) * y


def _moe(layer, x, mod, wr_t, rbias, wg, wu, wd, ln2):
    B, S, D = x.shape
    tiles_per_seq = S // TOKEN_TILE
    n_tiles = B * tiles_per_seq
    before = jnp.asarray(np.triu(np.ones((TOKEN_TILE, TOKEN_TILE), np.float32), k=1), dtype=BF16)

    def tile_spec(offset):
        def index(s):
            t = jnp.clip(s + offset, 0, n_tiles - 1)
            return (t // tiles_per_seq, t % tiles_per_seq, 0)
        return pl.BlockSpec((None, TOKEN_TILE, D), index)

    def mod_spec(offset):
        def index(s):
            t = jnp.clip(s + offset, 0, n_tiles - 1)
            return (layer, t // tiles_per_seq, 0, 0)
        return pl.BlockSpec((None, None, 6, D_MODEL), index)

    return pl.pallas_call(
        _moe_kernel,
        grid=(n_tiles + 1,),
        in_specs=[
            tile_spec(0),
            tile_spec(1),
            mod_spec(0),
            mod_spec(1),
            _resident(wr_t.shape),
            _resident(rbias.shape),
            _resident(before.shape),
            _resident_layer(wg, layer),
            _resident_layer(wu, layer),
            _resident_layer(wd, layer),
            _resident_layer(ln2, layer),
        ],
        out_specs=tile_spec(-1),
        out_shape=jax.ShapeDtypeStruct((B, S, D), F32),
        scratch_shapes=[pltpu.VMEM((TOKEN_TILE, D_MODEL), BF16),
                        pltpu.VMEM((TOKEN_TILE, D_MODEL), F32),
                        pltpu.VMEM((SORT_ROWS, D_MODEL), BF16),
                        pltpu.VMEM((SORT_ROWS, LANES), F32),
                        pltpu.VMEM((SORT_ROWS, D_MODEL), F32)],
        compiler_params=pltpu.CompilerParams(
            dimension_semantics=("arbitrary",),
            vmem_limit_bytes=VMEM_LIMIT_BYTES),
        name="grouped_moe",
    )(x, x, mod, mod, wr_t, rbias, before, wg, wu, wd, ln2)


def kernel(x, c, rel_bias_table, w_router, router_bias, w_ada, b_ada, w_in, sinks, sg_w, sg_b, sg_ln_g, sg_ln_b,
           w_br_attn, w_br_sg, w_out, ln1_g, ln1_b, w_gate, w_up, w_down, ln2_g, ln2_b):
    L = w_in.shape[0]
    B = x.shape[0]
    mod = _ada_modulation(c, w_ada, b_ada).reshape(L, B, 6, D_MODEL)
    bias = _band_bias(rel_bias_table)
    bias = bias.reshape(N_KV_HEADS, Q_PER_KV * BLOCK, 2 * BLOCK)

    w_in_b = w_in.astype(BF16)
    w_bra_b = w_br_attn.astype(BF16)
    w_brs_b = w_br_sg.astype(BF16)
    w_out_b = w_out.astype(BF16)
    wg_b = w_gate.astype(BF16)
    wu_b = w_up.astype(BF16)
    wd_b = w_down.reshape(L, N_EXPERT_GROUPS, EXPERTS_PER_GROUP * D_FF_EXPERT, D_MODEL).astype(BF16)
    wr_t = jnp.transpose(w_router).astype(BF16)
    rbias = router_bias.reshape(N_EXPERTS, 1)
    sink_col = jnp.repeat(sinks.reshape(L, N_KV_HEADS, Q_PER_KV), BLOCK, axis=2).reshape(
        L, N_KV_HEADS, Q_PER_KV * BLOCK, 1)
    sgw_pair = jnp.concatenate([sg_w[:, 0::2], sg_w[:, 1::2]], axis=-1).astype(BF16)
    sgb_full = jnp.repeat(jnp.swapaxes(sg_b, 1, 2), SG_GROUP_DIM, axis=2)
    sgln = jnp.stack([sg_ln_g, sg_ln_b], axis=1)
    ln1 = jnp.stack([ln1_g, ln1_b], axis=1)
    ln2 = jnp.stack([ln2_g, ln2_b], axis=1)
    for l in range(L):
        x = _token_mix(l, x, mod, w_in_b, w_bra_b, w_brs_b, w_out_b, bias, sink_col, sgw_pair, sgb_full, sgln, ln1)
        x = _moe(l, x, mod, wr_t, rbias, wg_b, wu_b, wd_b, ln2)
    return x
```

```python
import functools
import math

import numpy as np
import jax
import jax.numpy as jnp
from jax import lax
from jax.experimental import pallas as pl
from jax.experimental.pallas import tpu as pltpu

D_MODEL = 1024
DEPTH = 4
N_Q_HEADS = 8
N_KV_HEADS = 2
HEAD_DIM = 64
Q_PER_KV = N_Q_HEADS // N_KV_HEADS
WINDOW = 128
BLOCK = WINDOW
ATTN_WIDTH = N_Q_HEADS * HEAD_DIM
KV_WIDTH = N_KV_HEADS * HEAD_DIM
N_BUCKETS = 32
MAX_DISTANCE = 128
N_SG_GROUPS = 8
SG_GROUP_DIM = 64
SG_CHUNK = 128
SG_WIDTH = N_SG_GROUPS * SG_GROUP_DIM
IN_WIDTH = ATTN_WIDTH + 2 * KV_WIDTH + 2 * SG_WIDTH + 2 * D_MODEL
N_EXPERTS = 16
N_EXPERT_GROUPS = 4
EXPERTS_PER_GROUP = N_EXPERTS // N_EXPERT_GROUPS
TOP_K = 2
D_FF_EXPERT = 256
DEEPNORM_ALPHA = (2 * DEPTH) ** 0.25
LN_EPS = 1e-5
NEG_INF = -1e30

OFF_Q = 0
OFF_KV = ATTN_WIDTH
OFF_SU = ATTN_WIDTH + 2 * KV_WIDTH
OFF_SV = OFF_SU + SG_WIDTH
OFF_GA = OFF_SV + SG_WIDTH
OFF_GS = OFF_GA + D_MODEL

LANES = 128
TOKEN_TILE = 512
BLOCKS_PER_TILE = TOKEN_TILE // BLOCK
REST_SLAB = 256
VMEM_LIMIT_BYTES = 56 * 1024 * 1024

BF16 = jnp.bfloat16
F32 = jnp.float32


def _layer_norm(x):
    mu = jnp.mean(x, axis=-1, keepdims=True)
    xc = x - mu
    var = jnp.mean(xc * xc, axis=-1, keepdims=True)
    return xc * lax.rsqrt(var + LN_EPS)


def _dot(a, b):
    return jnp.dot(a, b, preferred_element_type=F32)


def _dot_nt(a, b):
    return lax.dot_general(a, b, (((1,), (1,)), ((), ())), preferred_element_type=F32)


def _gelu(x):
    return 0.5 * x * (1.0 + lax.erf(x * (1.0 / math.sqrt(2.0))))


def _sigmoid(x):
    return 0.5 * (jnp.tanh(0.5 * x) + 1.0)


def _silu(x):
    return x * _sigmoid(x)


ADA_TILE = 1536


def _ada_kernel(c_ref, w_ref, b_ref, o_ref):
    cond = _silu(c_ref[...]).astype(BF16)
    o_ref[...] = _dot(cond, w_ref[...].astype(BF16)) + b_ref[...]


def _ada_modulation(c, w_ada, b_ada):
    L, D, N = w_ada.shape
    B = c.shape[0]
    return pl.pallas_call(
        _ada_kernel,
        grid=(L, N // ADA_TILE),
        in_specs=[
            pl.BlockSpec((B, D), lambda l, n: (0, 0)),
            pl.BlockSpec((None, D, ADA_TILE), lambda l, n: (l, 0, n)),
            pl.BlockSpec((None, 1, ADA_TILE), lambda l, n: (l, 0, n)),
        ],
        out_specs=pl.BlockSpec((None, B, ADA_TILE), lambda l, n: (l, 0, n)),
        out_shape=jax.ShapeDtypeStruct((L, B, N), F32),
        compiler_params=pltpu.CompilerParams(
            dimension_semantics=("arbitrary", "arbitrary"),
            vmem_limit_bytes=VMEM_LIMIT_BYTES),
        name="ada_modulation",
    )(c, w_ada, b_ada.reshape(L, 1, N))


def _band_buckets():
    qi = np.arange(BLOCK)[:, None]
    kj = np.arange(2 * BLOCK)[None, :]
    rel = qi + BLOCK - kj
    n = np.maximum(rel, 0)
    max_exact = N_BUCKETS // 2
    nf = np.maximum(n, 1).astype(np.float32)
    large = max_exact + (np.log(nf / np.float32(max_exact)) / np.float32(math.log(MAX_DISTANCE / max_exact))
                         * np.float32(N_BUCKETS - max_exact)).astype(np.int32)
    large = np.minimum(large, N_BUCKETS - 1)
    bucket = np.where(n < max_exact, n, large).astype(np.int32)
    in_window = ((rel >= 0) & (rel < WINDOW)).astype(np.int32)
    return bucket, in_window


def _bias_kernel(table_ref, bucket_ref, window_ref, o_ref):
    bucket = bucket_ref[...]
    in_window = window_ref[...] > 0
    for head in range(N_Q_HEADS):
        acc = jnp.zeros((BLOCK, 2 * BLOCK), F32)
        for b in range(N_BUCKETS):
            acc = jnp.where(bucket == b, table_ref[b, head], acc)
        o_ref[head] = jnp.where(in_window, acc, NEG_INF)


def _band_bias(rel_bias_table):
    bucket, in_window = _band_buckets()
    return pl.pallas_call(
        _bias_kernel,
        in_specs=[
            pl.BlockSpec(memory_space=pltpu.SMEM),
            pl.BlockSpec(memory_space=pltpu.VMEM),
            pl.BlockSpec(memory_space=pltpu.VMEM),
        ],
        out_specs=pl.BlockSpec(memory_space=pltpu.VMEM),
        out_shape=jax.ShapeDtypeStruct((N_Q_HEADS, BLOCK, 2 * BLOCK), F32),
        name="band_bias",
    )(rel_bias_table, jnp.asarray(bucket), jnp.asarray(in_window))


ANCHOR_ROWS = 16


def _token_mix_kernel(xc_ref, xn_ref, modc_ref, modn_ref, w_in_ref, w_bra_ref, w_brs_ref, w_out_ref, bias_ref,
                      sink_ref, sgw_ref, sgb_ref, sgln_ref, ln1_ref, o_ref, h_ref, z_ref, kprev_ref, vprev_ref,
                      *, tiles_per_seq, n_tiles):
    T = TOKEN_TILE
    step = pl.program_id(0)
    j = jnp.minimum(step, n_tiles - 1) % tiles_per_seq
    cur = pl.multiple_of((step % 2) * T, T)
    nxt = pl.multiple_of(((step + 1) % 2) * T, T)

    def modulated_norm(x, mod_ref):
        return (_layer_norm(x) * (1.0 + mod_ref[1:2, :]) + mod_ref[0:1, :]).astype(BF16)

    @pl.when(step == 0)
    def _():
        z_ref[...] = jnp.zeros_like(z_ref)
        h_ref[...] = jnp.zeros_like(h_ref)
        h_ref[0:T, :] = modulated_norm(xc_ref[...], modc_ref)

    @pl.when(j == 0)
    def _():
        kprev_ref[...] = jnp.zeros_like(kprev_ref)
        vprev_ref[...] = jnp.zeros_like(vprev_ref)

    def h_tile():
        return h_ref[pl.ds(cur, T), :]

    side_rows = T // 4

    def neighbour_norms(c):
        rows = slice(c * side_rows, (c + 1) * side_rows)
        h_ref[pl.ds(nxt + c * side_rows, side_rows), :] = modulated_norm(xn_ref[rows, :], modn_ref)
        out = _layer_norm(z_ref[rows, :]) * ln1_ref[0:1, :] + ln1_ref[1:2, :]
        o_ref[rows, :] = out
        fold = functools.reduce(lambda a, b: a + b,
                                [out[r:r + ANCHOR_ROWS, :] for r in range(0, side_rows, ANCHOR_ROWS)])
        h_ref[2 * T:2 * T + ANCHOR_ROWS, :] = fold.astype(BF16)

    lane = lax.broadcasted_iota(jnp.int32, (1, LANES), 1)
    lo = lane < HEAD_DIM

    q = (_dot(h_tile(), w_in_ref[:, OFF_Q:OFF_Q + ATTN_WIDTH]) * (HEAD_DIM ** -0.5)).astype(BF16)
    kv = _dot(h_tile(), w_in_ref[:, OFF_KV:OFF_KV + 2 * KV_WIDTH]).astype(BF16)
    k_full = jnp.concatenate([kprev_ref[...], kv[:, :KV_WIDTH]], axis=0)
    v_full = jnp.concatenate([vprev_ref[...], kv[:, KV_WIDTH:]], axis=0)
    kprev_ref[...] = kv[TOKEN_TILE - BLOCK:, :KV_WIDTH]
    vprev_ref[...] = kv[TOKEN_TILE - BLOCK:, KV_WIDTH:]

    k_sw = pltpu.roll(k_full, HEAD_DIM, 1)
    k_dup = (jnp.where(lo, k_full, k_sw), jnp.where(lo, k_sw, k_full))
    one = jnp.ones((), BF16)
    v_ext = (jnp.where(lo, v_full, one), jnp.where(lo, one, v_full))

    n_rest = (IN_WIDTH - OFF_SU) // REST_SLAB
    rest = []

    def project_rest(count):
        for _ in range(count):
            if len(rest) % 3 == 2:
                neighbour_norms(len(rest) // 3)
            c0 = OFF_SU + len(rest) * REST_SLAB
            rest.append(_dot(h_tile(), w_in_ref[:, c0:c0 + REST_SLAB]))

    key_in_prev = lax.broadcasted_iota(jnp.int32, (1, 2 * BLOCK), 1) < BLOCK
    zero = jnp.zeros((), BF16)
    units = [(n, hk) for n in range(BLOCKS_PER_TILE) for hk in range(N_KV_HEADS)]

    def scores(n, hk):
        r0 = n * BLOCK
        qa = q[r0:r0 + BLOCK, hk * 256:hk * 256 + LANES]
        qb = q[r0:r0 + BLOCK, hk * 256 + LANES:hk * 256 + 2 * LANES]
        qs = jnp.concatenate([jnp.where(lo, qa, zero), jnp.where(lo, zero, qa),
                              jnp.where(lo, qb, zero), jnp.where(lo, zero, qb)], axis=0)
        s = _dot_nt(qs, k_dup[hk][r0:r0 + 2 * BLOCK]) + bias_ref[hk]
        if n == 0:
            s = jnp.where(jnp.logical_and(j == 0, key_in_prev), NEG_INF, s)
        return s

    def attend(s, n, hk):
        r0 = n * BLOCK
        sink = sink_ref[hk]
        m = jnp.maximum(jnp.max(s, axis=-1, keepdims=True), sink)
        p = jnp.exp(s - m).astype(BF16)
        pv = _dot(p, v_ext[hk][r0:r0 + 2 * BLOCK])
        pv_sw = pltpu.roll(pv, HEAD_DIM, 1)
        row_sum = jnp.where(lo, pv_sw, pv) if hk == 0 else jnp.where(lo, pv, pv_sw)
        out = pv / (row_sum + jnp.exp(sink - m))
        out_sw = pltpu.roll(out, HEAD_DIM, 1)
        slabs = []
        for pair in range(2):
            first = slice(2 * pair * BLOCK, (2 * pair + 1) * BLOCK)
            second = slice((2 * pair + 1) * BLOCK, (2 * pair + 2) * BLOCK)
            if hk == 0:
                slabs.append(jnp.where(lo, out[first], out_sw[second]))
            else:
                slabs.append(jnp.where(lo, out_sw[first], out[second]))
        return slabs

    n_sg_slabs = 2 * SG_WIDTH // REST_SLAB
    project_rest(n_sg_slabs)
    su = jnp.concatenate(rest[:n_sg_slabs // 2], axis=1)
    sv = jnp.concatenate(rest[n_sg_slabs // 2:n_sg_slabs], axis=1)
    u = _gelu(su)
    vn = (_layer_norm(_gelu(sv)) * sgln_ref[0:1, :] + sgln_ref[1:2, :]).astype(BF16)
    row = lax.broadcasted_iota(jnp.int32, (SG_CHUNK, 2 * SG_CHUNK), 0)
    col = lax.broadcasted_iota(jnp.int32, (SG_CHUNK, 2 * SG_CHUNK), 1)
    causal = (col & (SG_CHUNK - 1)) <= row
    mix_jobs = [(s_idx, c) for s_idx in range(SG_WIDTH // LANES) for c in range(TOKEN_TILE // SG_CHUNK)]
    w_pairs = {}
    mixed_blocks = {}

    def mix(count):
        for _ in range(count):
            s_idx, c = mix_jobs[len(mixed_blocks)]
            if s_idx not in w_pairs:
                w_pairs[s_idx] = jnp.where(causal, sgw_ref[s_idx], zero)
            slab = vn[c * SG_CHUNK:(c + 1) * SG_CHUNK, s_idx * LANES:(s_idx + 1) * LANES]
            rhs = jnp.concatenate([jnp.where(lo, slab, zero), jnp.where(lo, zero, slab)], axis=0)
            mixed_blocks[(s_idx, c)] = _dot(w_pairs[s_idx], rhs) + sgb_ref[:, s_idx * LANES:(s_idx + 1) * LANES]

    slabs = {}
    s_next = scores(*units[0])
    for i, (n, hk) in enumerate(units):
        s_cur = s_next
        project_rest(1)
        if i + 1 < len(units):
            s_next = scores(*units[i + 1])
        mix(len(mix_jobs) // len(units))
        slabs[(n, hk)] = attend(s_cur, n, hk)
    mix(len(mix_jobs) - len(mixed_blocks))
    mixed = jnp.concatenate(
        [jnp.concatenate([mixed_blocks[(s_idx, c)] for c in range(TOKEN_TILE // SG_CHUNK)], axis=0)
         for s_idx in range(SG_WIDTH // LANES)], axis=1)
    sg = (u * mixed).astype(BF16)
    s_br = _dot(sg, w_brs_ref[...])
    attn = jnp.concatenate(
        [jnp.concatenate(slabs[(n, 0)] + slabs[(n, 1)], axis=1).astype(BF16) for n in range(BLOCKS_PER_TILE)],
        axis=0)
    a_br = _dot(attn, w_bra_ref[...])
    project_rest(n_rest - len(rest))
    ga = jnp.concatenate(rest[n_sg_slabs:n_sg_slabs + D_MODEL // REST_SLAB], axis=1)
    gs = jnp.concatenate(rest[n_sg_slabs + D_MODEL // REST_SLAB:], axis=1)

    merged = (_sigmoid(ga) * a_br + _sigmoid(gs) * s_br).astype(BF16)
    y = _dot(merged, w_out_ref[...])
    z_ref[...] = DEEPNORM_ALPHA * xc_ref[...] + (1.0 + modc_ref[2:3, :]) * y


def _resident(shape):
    ndim = len(shape)
    return pl.BlockSpec(shape, lambda *_: (0,) * ndim, pipeline_mode=pl.Buffered(1))


def _resident_layer(stacked, layer):
    shape = stacked.shape[1:]
    ndim = len(shape)
    return pl.BlockSpec((None,) + shape, lambda *_: (layer,) + (0,) * ndim, pipeline_mode=pl.Buffered(1))


def _tile_specs(x, layer):
    B, S, D = x.shape
    tiles_per_seq = S // TOKEN_TILE
    n_tiles = B * tiles_per_seq

    def tile_spec(offset):
        def index(s):
            t = jnp.clip(s + offset, 0, n_tiles - 1)
            return (t // tiles_per_seq, t % tiles_per_seq, 0)
        return pl.BlockSpec((None, TOKEN_TILE, D), index)

    def mod_spec(offset):
        def index(s):
            t = jnp.clip(s + offset, 0, n_tiles - 1)
            return (layer, t // tiles_per_seq, 0, 0)
        return pl.BlockSpec((None, None, 6, D_MODEL), index)

    return tile_spec, mod_spec, tiles_per_seq, n_tiles


def _token_mix(layer, x, mod, w_in, w_bra, w_brs, w_out, bias, sink_col, sgw_pair, sgb_full, sgln, ln1):
    B, S, D = x.shape
    tile_spec, mod_spec, tiles_per_seq, n_tiles = _tile_specs(x, layer)
    return pl.pallas_call(
        functools.partial(_token_mix_kernel, tiles_per_seq=tiles_per_seq, n_tiles=n_tiles),
        grid=(n_tiles + 1,),
        in_specs=[
            tile_spec(0),
            tile_spec(1),
            mod_spec(0),
            mod_spec(1),
            _resident_layer(w_in, layer),
            _resident_layer(w_bra, layer),
            _resident_layer(w_brs, layer),
            _resident_layer(w_out, layer),
            _resident(bias.shape),
            _resident_layer(sink_col, layer),
            _resident_layer(sgw_pair, layer),
            _resident_layer(sgb_full, layer),
            _resident_layer(sgln, layer),
            _resident_layer(ln1, layer),
        ],
        out_specs=tile_spec(-1),
        out_shape=jax.ShapeDtypeStruct((B, S, D), F32),
        scratch_shapes=[pltpu.VMEM((2 * TOKEN_TILE + ANCHOR_ROWS, D_MODEL), BF16),
                        pltpu.VMEM((TOKEN_TILE, D_MODEL), F32),
                        pltpu.VMEM((BLOCK, KV_WIDTH), BF16), pltpu.VMEM((BLOCK, KV_WIDTH), BF16)],
        compiler_params=pltpu.CompilerParams(
            dimension_semantics=("arbitrary",),
            vmem_limit_bytes=VMEM_LIMIT_BYTES),
        name="token_mix",
    )(x, x, mod, mod, w_in, w_bra, w_brs, w_out, bias, sink_col, sgw_pair, sgb_full, sgln, ln1)


def _route(logits_t, rbias):
    rows = [logits_t[e:e + 1, :] for e in range(N_EXPERTS)]
    m = functools.reduce(jnp.maximum, rows)
    ex = [jnp.exp(r - m) for r in rows]
    denom = functools.reduce(lambda a, b: a + b, ex)
    probs = [e_ / denom for e_ in ex]
    sel = [probs[e] + rbias[e:e + 1, :] for e in range(N_EXPERTS)]

    def beats(a, b, a_first):
        return (a >= b) if a_first else (a > b)

    chosen = []
    score = []
    for g in range(N_EXPERT_GROUPS):
        members = list(range(g * EXPERTS_PER_GROUP, (g + 1) * EXPERTS_PER_GROUP))
        total = None
        for i in members:
            rank = None
            for o in members:
                if o == i:
                    continue
                ahead = beats(sel[o], sel[i], o < i).astype(F32)
                rank = ahead if rank is None else rank + ahead
            pick = rank < TOP_K
            chosen.append(pick)
            term = jnp.where(pick, sel[i], 0.0)
            total = term if total is None else total + term
        score.append(total)
    best = []
    for g in range(N_EXPERT_GROUPS):
        is_best = None
        for o in range(N_EXPERT_GROUPS):
            if o == g:
                continue
            wins = beats(score[g], score[o], g < o)
            is_best = wins if is_best is None else jnp.logical_and(is_best, wins)
        best.append(is_best)
    picked = [jnp.where(jnp.logical_and(chosen[e], best[e // EXPERTS_PER_GROUP]), probs[e], 0.0)
              for e in range(N_EXPERTS)]
    norm = functools.reduce(lambda a, b: a + b, picked)
    return [p_ / norm for p_ in picked], best


SUB_TILE = 192
SEG_ALIGN = 16
SORT_SLOTS = TOKEN_TILE + N_EXPERT_GROUPS * SEG_ALIGN
SORT_ROWS = SORT_SLOTS + SUB_TILE
UNSORT_COLS = -(-SORT_SLOTS // LANES) * LANES
POS_LANE = N_EXPERTS


def _expert_tile(hs_ref, cs_ref, wg_ref, wu_ref, wd_ref, g, start):
    hs = hs_ref[pl.ds(start, SUB_TILE), :]
    cw = cs_ref[pl.ds(start, SUB_TILE), :]
    experts = range(g * EXPERTS_PER_GROUP, (g + 1) * EXPERTS_PER_GROUP)
    gate = jnp.concatenate([_dot(hs, wg_ref[e]) for e in experts], axis=1)
    up = jnp.concatenate([_dot(hs, wu_ref[e]) for e in experts], axis=1)
    scale = jnp.concatenate(
        [jnp.broadcast_to(cw[:, e:e + 1], (SUB_TILE, D_FF_EXPERT)) for e in experts], axis=1)
    hid = (_silu(gate) * up * scale).astype(BF16)
    return _dot(hid, wd_ref[g])


def _moe_kernel(xc_ref, xn_ref, modc_ref, modn_ref, wr_ref, rb_ref, before_ref, wg_ref, wu_ref, wd_ref, ln2_ref,
                o_ref, h_ref, z_ref, hs_ref, cs_ref, ys_ref):
    T = TOKEN_TILE

    def modulated_norm(x_ref, mod_ref):
        return (_layer_norm(x_ref[...]) * (1.0 + mod_ref[4:5, :]) + mod_ref[3:4, :]).astype(BF16)

    @pl.when(pl.program_id(0) == 0)
    def _():
        hs_ref[SORT_SLOTS:, :] = jnp.zeros((SORT_ROWS - SORT_SLOTS, D_MODEL), BF16)
        cs_ref[SORT_SLOTS:, :] = jnp.zeros((SORT_ROWS - SORT_SLOTS, LANES), F32)
        ys_ref[...] = jnp.zeros_like(ys_ref)
        z_ref[...] = jnp.zeros_like(z_ref)
        h_ref[...] = modulated_norm(xc_ref, modc_ref)

    h = h_ref[...]
    logits_t = _dot_nt(wr_ref[...], h)
    comb_rows, best = _route(logits_t, rb_ref[...])

    onehot = jnp.concatenate([b.astype(F32) for b in best] + [jnp.zeros((8 - N_EXPERT_GROUPS, T), F32)], axis=0)
    rank = _dot(onehot.astype(BF16), before_ref[...])
    counts = [jnp.sum(onehot[g:g + 1, :]).astype(jnp.int32) for g in range(N_EXPERT_GROUPS)]
    bases = []
    acc = jnp.int32(0)
    for g in range(N_EXPERT_GROUPS):
        bases.append(acc)
        acc = acc + ((counts[g] + (SEG_ALIGN - 1)) // SEG_ALIGN) * SEG_ALIGN
    pos_f = functools.reduce(lambda a, b: a + b,
                             [onehot[g:g + 1, :] * (rank[g:g + 1, :] + bases[g].astype(F32))
                              for g in range(N_EXPERT_GROUPS)])
    pos = pos_f.astype(jnp.int32)

    sort_mat = (lax.broadcasted_iota(jnp.int32, (SORT_SLOTS, T), 0) == pos).astype(BF16)
    hs_ref[:SORT_SLOTS, :] = _dot(sort_mat, h).astype(BF16)

    tok_t = jnp.concatenate(comb_rows + [pos_f, jnp.zeros((LANES - N_EXPERTS - 1, T), F32)], axis=0)
    tok = jnp.transpose(tok_t)
    hi = tok.astype(BF16)
    r1 = tok - hi.astype(F32)
    mid = r1.astype(BF16)
    lo = (r1 - mid.astype(F32)).astype(BF16)
    parts = _dot(sort_mat, jnp.concatenate([hi, mid, lo], axis=1))
    cs_ref[:SORT_SLOTS, :] = parts[:, :LANES] + parts[:, LANES:2 * LANES] + parts[:, 2 * LANES:]

    ys_ref[T:UNSORT_COLS, :] = jnp.zeros((UNSORT_COLS - T, D_MODEL), F32)

    rows_per_group = T // N_EXPERT_GROUPS
    for g in range(N_EXPERT_GROUPS):
        rows = slice(g * rows_per_group, (g + 1) * rows_per_group)
        h_ref[rows, :] = (_layer_norm(xn_ref[rows, :]) * (1.0 + modn_ref[4:5, :]) + modn_ref[3:4, :]).astype(BF16)
        o_ref[rows, :] = _layer_norm(z_ref[rows, :]) * ln2_ref[0:1, :] + ln2_ref[1:2, :]
        start = pl.multiple_of(bases[g], SEG_ALIGN)
        ys_ref[pl.ds(start, SUB_TILE), :] = _expert_tile(hs_ref, cs_ref, wg_ref, wu_ref, wd_ref, g, start)

        def body(k, carry, g=g):
            start = pl.multiple_of(bases[g] + k * SUB_TILE, SEG_ALIGN)
            ys_ref[pl.ds(start, SUB_TILE), :] = _expert_tile(hs_ref, cs_ref, wg_ref, wu_ref, wd_ref, g, start)
            return carry
        lax.fori_loop(1, (counts[g] + (SUB_TILE - 1)) // SUB_TILE, body, 0)

    unsort_mat = (tok[:, POS_LANE:POS_LANE + 1].astype(jnp.int32)
                  == lax.broadcasted_iota(jnp.int32, (T, UNSORT_COLS), 1)).astype(BF16)
    y = _dot(unsort_mat, ys_ref[:UNSORT_COLS, :].astype(BF16))
    z_ref[...] = DEEPNORM_ALPHA * xc_ref[...] + (1.0 + modc_ref[5:6, :]) * y


def _moe(layer, x, mod, wr_t, rbias, wg, wu, wd, ln2):
    B, S, D = x.shape
    tile_spec, mod_spec, _, n_tiles = _tile_specs(x, layer)
    before = jnp.asarray(np.triu(np.ones((TOKEN_TILE, TOKEN_TILE), np.float32), k=1), dtype=BF16)

    return pl.pallas_call(
        _moe_kernel,
        grid=(n_tiles + 1,),
        in_specs=[
            tile_spec(0),
            tile_spec(1),
            mod_spec(0),
            mod_spec(1),
            _resident(wr_t.shape),
            _resident(rbias.shape),
            _resident(before.shape),
            _resident_layer(wg, layer),
            _resident_layer(wu, layer),
            _resident_layer(wd, layer),
            _resident_layer(ln2, layer),
        ],
        out_specs=tile_spec(-1),
        out_shape=jax.ShapeDtypeStruct((B, S, D), F32),
        scratch_shapes=[pltpu.VMEM((TOKEN_TILE, D_MODEL), BF16),
                        pltpu.VMEM((TOKEN_TILE, D_MODEL), F32),
                        pltpu.VMEM((SORT_ROWS, D_MODEL), BF16),
                        pltpu.VMEM((SORT_ROWS, LANES), F32),
                        pltpu.VMEM((SORT_ROWS, D_MODEL), F32)],
        compiler_params=pltpu.CompilerParams(
            dimension_semantics=("arbitrary",),
            vmem_limit_bytes=VMEM_LIMIT_BYTES),
        name="grouped_moe",
    )(x, x, mod, mod, wr_t, rbias, before, wg, wu, wd, ln2)


def kernel(x, c, rel_bias_table, w_router, router_bias, w_ada, b_ada, w_in, sinks, sg_w, sg_b, sg_ln_g, sg_ln_b,
           w_br_attn, w_br_sg, w_out, ln1_g, ln1_b, w_gate, w_up, w_down, ln2_g, ln2_b):
    L = w_in.shape[0]
    B = x.shape[0]
    mod = _ada_modulation(c, w_ada, b_ada).reshape(L, B, 6, D_MODEL)
    bias = _band_bias(rel_bias_table)
    bias = bias.reshape(N_KV_HEADS, Q_PER_KV * BLOCK, 2 * BLOCK)

    w_in_b = w_in.astype(BF16)
    w_bra_b = w_br_attn.astype(BF16)
    w_brs_b = w_br_sg.astype(BF16)
    w_out_b = w_out.astype(BF16)
    wg_b = w_gate.astype(BF16)
    wu_b = w_up.astype(BF16)
    wd_b = w_down.reshape(L, N_EXPERT_GROUPS, EXPERTS_PER_GROUP * D_FF_EXPERT, D_MODEL).astype(BF16)
    wr_t = jnp.transpose(w_router).astype(BF16)
    rbias = router_bias.reshape(N_EXPERTS, 1)
    sink_col = jnp.repeat(sinks.reshape(L, N_KV_HEADS, Q_PER_KV), BLOCK, axis=2).reshape(
        L, N_KV_HEADS, Q_PER_KV * BLOCK, 1)
    sgw_pair = jnp.concatenate([sg_w[:, 0::2], sg_w[:, 1::2]], axis=-1).astype(BF16)
    sgb_full = jnp.repeat(jnp.swapaxes(sg_b, 1, 2), SG_GROUP_DIM, axis=2)
    sgln = jnp.stack([sg_ln_g, sg_ln_b], axis=1)
    ln1 = jnp.stack([ln1_g, ln1_b], axis=1)
    ln2 = jnp.stack([ln2_g, ln2_b], axis=1)
    for l in range(L):
        x = _token_mix(l, x, mod, w_in_b, w_bra_b, w_brs_b, w_out_b, bias, sink_col, sgw_pair, sgb_full, sgln, ln1)
        x = _moe(l, x, mod, wr_t, rbias, wg_b, wu_b, wd_b, ln2)
    return x
```

```python
import functools
import math

import numpy as np
import jax
import jax.numpy as jnp
from jax import lax
from jax.experimental import pallas as pl
from jax.experimental.pallas import tpu as pltpu

D_MODEL = 1024
DEPTH = 4
N_Q_HEADS = 8
N_KV_HEADS = 2
HEAD_DIM = 64
Q_PER_KV = N_Q_HEADS // N_KV_HEADS
WINDOW = 128
BLOCK = WINDOW
ATTN_WIDTH = N_Q_HEADS * HEAD_DIM
KV_WIDTH = N_KV_HEADS * HEAD_DIM
N_BUCKETS = 32
MAX_DISTANCE = 128
N_SG_GROUPS = 8
SG_GROUP_DIM = 64
SG_CHUNK = 128
SG_WIDTH = N_SG_GROUPS * SG_GROUP_DIM
IN_WIDTH = ATTN_WIDTH + 2 * KV_WIDTH + 2 * SG_WIDTH + 2 * D_MODEL
N_EXPERTS = 16
N_EXPERT_GROUPS = 4
EXPERTS_PER_GROUP = N_EXPERTS // N_EXPERT_GROUPS
TOP_K = 2
D_FF_EXPERT = 256
DEEPNORM_ALPHA = (2 * DEPTH) ** 0.25
LN_EPS = 1e-5
NEG_INF = -1e30

OFF_Q = 0
OFF_KV = ATTN_WIDTH
OFF_SU = ATTN_WIDTH + 2 * KV_WIDTH
OFF_SV = OFF_SU + SG_WIDTH
OFF_GA = OFF_SV + SG_WIDTH
OFF_GS = OFF_GA + D_MODEL

LANES = 128
TOKEN_TILE = 512
BLOCKS_PER_TILE = TOKEN_TILE // BLOCK
REST_SLAB = 256
VMEM_LIMIT_BYTES = 56 * 1024 * 1024

BF16 = jnp.bfloat16
F32 = jnp.float32


def _layer_norm(x):
    mu = jnp.mean(x, axis=-1, keepdims=True)
    xc = x - mu
    var = jnp.mean(xc * xc, axis=-1, keepdims=True)
    return xc * lax.rsqrt(var + LN_EPS)


def _dot(a, b):
    return jnp.dot(a, b, preferred_element_type=F32)


def _dot_nt(a, b):
    return lax.dot_general(a, b, (((1,), (1,)), ((), ())), preferred_element_type=F32)


def _gelu(x):
    return 0.5 * x * (1.0 + lax.erf(x * (1.0 / math.sqrt(2.0))))


def _sigmoid(x):
    return 0.5 * (jnp.tanh(0.5 * x) + 1.0)


def _silu(x):
    return x * _sigmoid(x)


ADA_TILE = 1536


def _ada_kernel(c_ref, w_ref, b_ref, o_ref):
    cond = _silu(c_ref[...]).astype(BF16)
    o_ref[...] = _dot(cond, w_ref[...].astype(BF16)) + b_ref[...]


def _ada_modulation(c, w_ada, b_ada):
    L, D, N = w_ada.shape
    B = c.shape[0]
    return pl.pallas_call(
        _ada_kernel,
        grid=(L, N // ADA_TILE),
        in_specs=[
            pl.BlockSpec((B, D), lambda l, n: (0, 0)),
            pl.BlockSpec((None, D, ADA_TILE), lambda l, n: (l, 0, n)),
            pl.BlockSpec((None, 1, ADA_TILE), lambda l, n: (l, 0, n)),
        ],
        out_specs=pl.BlockSpec((None, B, ADA_TILE), lambda l, n: (l, 0, n)),
        out_shape=jax.ShapeDtypeStruct((L, B, N), F32),
        compiler_params=pltpu.CompilerParams(
            dimension_semantics=("arbitrary", "arbitrary"),
            vmem_limit_bytes=VMEM_LIMIT_BYTES),
        name="ada_modulation",
    )(c, w_ada, b_ada.reshape(L, 1, N))


def _band_buckets():
    qi = np.arange(BLOCK)[:, None]
    kj = np.arange(2 * BLOCK)[None, :]
    rel = qi + BLOCK - kj
    n = np.maximum(rel, 0)
    max_exact = N_BUCKETS // 2
    nf = np.maximum(n, 1).astype(np.float32)
    large = max_exact + (np.log(nf / np.float32(max_exact)) / np.float32(math.log(MAX_DISTANCE / max_exact))
                         * np.float32(N_BUCKETS - max_exact)).astype(np.int32)
    large = np.minimum(large, N_BUCKETS - 1)
    bucket = np.where(n < max_exact, n, large).astype(np.int32)
    in_window = ((rel >= 0) & (rel < WINDOW)).astype(np.int32)
    return bucket, in_window


def _bias_kernel(table_ref, bucket_ref, window_ref, o_ref):
    bucket = bucket_ref[...]
    in_window = window_ref[...] > 0
    for head in range(N_Q_HEADS):
        acc = jnp.zeros((BLOCK, 2 * BLOCK), F32)
        for b in range(N_BUCKETS):
            acc = jnp.where(bucket == b, table_ref[b, head], acc)
        o_ref[head] = jnp.where(in_window, acc, NEG_INF)


def _band_bias(rel_bias_table):
    bucket, in_window = _band_buckets()
    return pl.pallas_call(
        _bias_kernel,
        in_specs=[
            pl.BlockSpec(memory_space=pltpu.SMEM),
            pl.BlockSpec(memory_space=pltpu.VMEM),
            pl.BlockSpec(memory_space=pltpu.VMEM),
        ],
        out_specs=pl.BlockSpec(memory_space=pltpu.VMEM),
        out_shape=jax.ShapeDtypeStruct((N_Q_HEADS, BLOCK, 2 * BLOCK), F32),
        name="band_bias",
    )(rel_bias_table, jnp.asarray(bucket), jnp.asarray(in_window))


ANCHOR_ROWS = 16


def _token_mix_kernel(xc_ref, xn_ref, modc_ref, modn_ref, w_in_ref, w_bra_ref, w_brs_ref, w_out_ref, bias_ref,
                      sink_ref, sgw_ref, sgb_ref, sgln_ref, ln1_ref, o_ref, h_ref, z_ref, kprev_ref, vprev_ref,
                      *, tiles_per_seq, n_tiles):
    T = TOKEN_TILE
    step = pl.program_id(0)
    j = jnp.minimum(step, n_tiles - 1) % tiles_per_seq
    cur = pl.multiple_of((step % 2) * T, T)
    nxt = pl.multiple_of(((step + 1) % 2) * T, T)

    def modulated_norm(x, mod_ref):
        return (_layer_norm(x) * (1.0 + mod_ref[1:2, :]) + mod_ref[0:1, :]).astype(BF16)

    @pl.when(step == 0)
    def _():
        z_ref[...] = jnp.zeros_like(z_ref)
        h_ref[...] = jnp.zeros_like(h_ref)
        h_ref[0:T, :] = modulated_norm(xc_ref[...], modc_ref)

    @pl.when(j == 0)
    def _():
        kprev_ref[...] = jnp.zeros_like(kprev_ref)
        vprev_ref[...] = jnp.zeros_like(vprev_ref)

    def h_tile():
        return h_ref[pl.ds(cur, T), :]

    side_rows = T // 4

    def neighbour_norms(c):
        rows = slice(c * side_rows, (c + 1) * side_rows)
        h_ref[pl.ds(nxt + c * side_rows, side_rows), :] = modulated_norm(xn_ref[rows, :], modn_ref)
        out = _layer_norm(z_ref[rows, :]) * ln1_ref[0:1, :] + ln1_ref[1:2, :]
        o_ref[rows, :] = out
        fold = functools.reduce(lambda a, b: a + b,
                                [out[r:r + ANCHOR_ROWS, :] for r in range(0, side_rows, ANCHOR_ROWS)])
        h_ref[2 * T:2 * T + ANCHOR_ROWS, :] = fold.astype(BF16)

    lane = lax.broadcasted_iota(jnp.int32, (1, LANES), 1)
    lo = lane < HEAD_DIM

    q = (_dot(h_tile(), w_in_ref[:, OFF_Q:OFF_Q + ATTN_WIDTH]) * (HEAD_DIM ** -0.5)).astype(BF16)
    kv = _dot(h_tile(), w_in_ref[:, OFF_KV:OFF_KV + 2 * KV_WIDTH]).astype(BF16)
    k_full = jnp.concatenate([kprev_ref[...], kv[:, :KV_WIDTH]], axis=0)
    v_full = jnp.concatenate([vprev_ref[...], kv[:, KV_WIDTH:]], axis=0)
    kprev_ref[...] = kv[TOKEN_TILE - BLOCK:, :KV_WIDTH]
    vprev_ref[...] = kv[TOKEN_TILE - BLOCK:, KV_WIDTH:]

    k_sw = pltpu.roll(k_full, HEAD_DIM, 1)
    k_dup = (jnp.where(lo, k_full, k_sw), jnp.where(lo, k_sw, k_full))
    one = jnp.ones((), BF16)
    v_ext = (jnp.where(lo, v_full, one), jnp.where(lo, one, v_full))

    n_rest = (IN_WIDTH - OFF_SU) // REST_SLAB
    rest = []

    def project_rest(count):
        for _ in range(count):
            if len(rest) % 3 == 2:
                neighbour_norms(len(rest) // 3)
            c0 = OFF_SU + len(rest) * REST_SLAB
            rest.append(_dot(h_tile(), w_in_ref[:, c0:c0 + REST_SLAB]))

    key_in_prev = lax.broadcasted_iota(jnp.int32, (1, 2 * BLOCK), 1) < BLOCK
    zero = jnp.zeros((), BF16)
    units = [(n, hk) for n in range(BLOCKS_PER_TILE) for hk in range(N_KV_HEADS)]

    def scores(n, hk):
        r0 = n * BLOCK
        qa = q[r0:r0 + BLOCK, hk * 256:hk * 256 + LANES]
        qb = q[r0:r0 + BLOCK, hk * 256 + LANES:hk * 256 + 2 * LANES]
        qs = jnp.concatenate([jnp.where(lo, qa, zero), jnp.where(lo, zero, qa),
                              jnp.where(lo, qb, zero), jnp.where(lo, zero, qb)], axis=0)
        s = _dot_nt(qs, k_dup[hk][r0:r0 + 2 * BLOCK]) + bias_ref[hk]
        if n == 0:
            s = jnp.where(jnp.logical_and(j == 0, key_in_prev), NEG_INF, s)
        return s

    def attend(s, n, hk):
        r0 = n * BLOCK
        sink = sink_ref[hk]
        m = jnp.maximum(jnp.max(s, axis=-1, keepdims=True), sink)
        p = jnp.exp(s - m).astype(BF16)
        pv = _dot(p, v_ext[hk][r0:r0 + 2 * BLOCK])
        pv_sw = pltpu.roll(pv, HEAD_DIM, 1)
        row_sum = jnp.where(lo, pv_sw, pv) if hk == 0 else jnp.where(lo, pv, pv_sw)
        out = pv / (row_sum + jnp.exp(sink - m))
        out_sw = pltpu.roll(out, HEAD_DIM, 1)
        slabs = []
        for pair in range(2):
            first = slice(2 * pair * BLOCK, (2 * pair + 1) * BLOCK)
            second = slice((2 * pair + 1) * BLOCK, (2 * pair + 2) * BLOCK)
            if hk == 0:
                slabs.append(jnp.where(lo, out[first], out_sw[second]))
            else:
                slabs.append(jnp.where(lo, out_sw[first], out[second]))
        return slabs

    n_sg_slabs = 2 * SG_WIDTH // REST_SLAB
    project_rest(n_sg_slabs)
    su = jnp.concatenate(rest[:n_sg_slabs // 2], axis=1)
    sv = jnp.concatenate(rest[n_sg_slabs // 2:n_sg_slabs], axis=1)
    u = _gelu(su)
    vn = (_layer_norm(_gelu(sv)) * sgln_ref[0:1, :] + sgln_ref[1:2, :]).astype(BF16)
    row = lax.broadcasted_iota(jnp.int32, (SG_CHUNK, 2 * SG_CHUNK), 0)
    col = lax.broadcasted_iota(jnp.int32, (SG_CHUNK, 2 * SG_CHUNK), 1)
    causal = (col & (SG_CHUNK - 1)) <= row
    mix_jobs = [(s_idx, c) for s_idx in range(SG_WIDTH // LANES) for c in range(TOKEN_TILE // SG_CHUNK)]
    w_pairs = {}
    mixed_blocks = {}

    def mix(count):
        for _ in range(count):
            s_idx, c = mix_jobs[len(mixed_blocks)]
            if s_idx not in w_pairs:
                w_pairs[s_idx] = jnp.where(causal, sgw_ref[s_idx], zero)
            slab = vn[c * SG_CHUNK:(c + 1) * SG_CHUNK, s_idx * LANES:(s_idx + 1) * LANES]
            rhs = jnp.concatenate([jnp.where(lo, slab, zero), jnp.where(lo, zero, slab)], axis=0)
            mixed_blocks[(s_idx, c)] = _dot(w_pairs[s_idx], rhs) + sgb_ref[:, s_idx * LANES:(s_idx + 1) * LANES]

    slabs = {}
    s_next = scores(*units[0])
    for i, (n, hk) in enumerate(units):
        s_cur = s_next
        project_rest(1)
        if i + 1 < len(units):
            s_next = scores(*units[i + 1])
        mix(len(mix_jobs) // len(units))
        slabs[(n, hk)] = attend(s_cur, n, hk)
    mix(len(mix_jobs) - len(mixed_blocks))
    mixed = jnp.concatenate(
        [jnp.concatenate([mixed_blocks[(s_idx, c)] for c in range(TOKEN_TILE // SG_CHUNK)], axis=0)
         for s_idx in range(SG_WIDTH // LANES)], axis=1)
    sg = (u * mixed).astype(BF16)
    s_br = _dot(sg, w_brs_ref[...])
    attn = jnp.concatenate(
        [jnp.concatenate(slabs[(n, 0)] + slabs[(n, 1)], axis=1).astype(BF16) for n in range(BLOCKS_PER_TILE)],
        axis=0)
    a_br = _dot(attn, w_bra_ref[...])
    project_rest(n_rest - len(rest))
    ga = jnp.concatenate(rest[n_sg_slabs:n_sg_slabs + D_MODEL // REST_SLAB], axis=1)
    gs = jnp.concatenate(rest[n_sg_slabs + D_MODEL // REST_SLAB:], axis=1)

    merged = (_sigmoid(ga) * a_br + _sigmoid(gs) * s_br).astype(BF16)
    y = _dot(merged, w_out_ref[...])
    z_ref[...] = DEEPNORM_ALPHA * xc_ref[...] + (1.0 + modc_ref[2:3, :]) * y


def _resident(shape):
    ndim = len(shape)
    return pl.BlockSpec(shape, lambda *_: (0,) * ndim, pipeline_mode=pl.Buffered(1))


def _resident_layer(stacked, layer):
    shape = stacked.shape[1:]
    ndim = len(shape)
    return pl.BlockSpec((None,) + shape, lambda *_: (layer,) + (0,) * ndim, pipeline_mode=pl.Buffered(1))


def _tile_specs(x, layer):
    B, S, D = x.shape
    tiles_per_seq = S // TOKEN_TILE
    n_tiles = B * tiles_per_seq

    def tile_spec(offset):
        def index(s):
            t = jnp.clip(s + offset, 0, n_tiles - 1)
            return (t // tiles_per_seq, t % tiles_per_seq, 0)
        return pl.BlockSpec((None, TOKEN_TILE, D), index)

    def mod_spec(offset):
        def index(s):
            t = jnp.clip(s + offset, 0, n_tiles - 1)
            return (layer, t // tiles_per_seq, 0, 0)
        return pl.BlockSpec((None, None, 6, D_MODEL), index)

    return tile_spec, mod_spec, tiles_per_seq, n_tiles


def _token_mix(layer, x, mod, w_in, w_bra, w_brs, w_out, bias, sink_col, sgw_pair, sgb_full, sgln, ln1):
    B, S, D = x.shape
    tile_spec, mod_spec, tiles_per_seq, n_tiles = _tile_specs(x, layer)
    return pl.pallas_call(
        functools.partial(_token_mix_kernel, tiles_per_seq=tiles_per_seq, n_tiles=n_tiles),
        grid=(n_tiles + 1,),
        in_specs=[
            tile_spec(0),
            tile_spec(1),
            mod_spec(0),
            mod_spec(1),
            _resident_layer(w_in, layer),
            _resident_layer(w_bra, layer),
            _resident_layer(w_brs, layer),
            _resident_layer(w_out, layer),
            _resident(bias.shape),
            _resident_layer(sink_col, layer),
            _resident_layer(sgw_pair, layer),
            _resident_layer(sgb_full, layer),
            _resident_layer(sgln, layer),
            _resident_layer(ln1, layer),
        ],
        out_specs=tile_spec(-1),
        out_shape=jax.ShapeDtypeStruct((B, S, D), F32),
        scratch_shapes=[pltpu.VMEM((2 * TOKEN_TILE + ANCHOR_ROWS, D_MODEL), BF16),
                        pltpu.VMEM((TOKEN_TILE, D_MODEL), F32),
                        pltpu.VMEM((BLOCK, KV_WIDTH), BF16), pltpu.VMEM((BLOCK, KV_WIDTH), BF16)],
        compiler_params=pltpu.CompilerParams(
            dimension_semantics=("arbitrary",),
            vmem_limit_bytes=VMEM_LIMIT_BYTES),
        name="token_mix",
    )(x, x, mod, mod, w_in, w_bra, w_brs, w_out, bias, sink_col, sgw_pair, sgb_full, sgln, ln1)


def _route(logits_t, rbias):
    rows = [logits_t[e:e + 1, :] for e in range(N_EXPERTS)]
    m = functools.reduce(jnp.maximum, rows)
    ex = [jnp.exp(r - m) for r in rows]
    denom = functools.reduce(lambda a, b: a + b, ex)
    probs = [e_ / denom for e_ in ex]
    sel = [probs[e] + rbias[e:e + 1, :] for e in range(N_EXPERTS)]

    def beats(a, b, a_first):
        return (a >= b) if a_first else (a > b)

    chosen = []
    score = []
    for g in range(N_EXPERT_GROUPS):
        members = list(range(g * EXPERTS_PER_GROUP, (g + 1) * EXPERTS_PER_GROUP))
        total = None
        for i in members:
            rank = None
            for o in members:
                if o == i:
                    continue
                ahead = beats(sel[o], sel[i], o < i).astype(F32)
                rank = ahead if rank is None else rank + ahead
            pick = rank < TOP_K
            chosen.append(pick)
            term = jnp.where(pick, sel[i], 0.0)
            total = term if total is None else total + term
        score.append(total)
    best = []
    for g in range(N_EXPERT_GROUPS):
        is_best = None
        for o in range(N_EXPERT_GROUPS):
            if o == g:
                continue
            wins = beats(score[g], score[o], g < o)
            is_best = wins if is_best is None else jnp.logical_and(is_best, wins)
        best.append(is_best)
    picked = [jnp.where(jnp.logical_and(chosen[e], best[e // EXPERTS_PER_GROUP]), probs[e], 0.0)
              for e in range(N_EXPERTS)]
    norm = functools.reduce(lambda a, b: a + b, picked)
    return [p_ / norm for p_ in picked], best


N_QUARTERS = 4
QUARTER = TOKEN_TILE // N_QUARTERS
SEG_ALIGN = 16
QUARTER_SLOTS = QUARTER + N_EXPERT_GROUPS * SEG_ALIGN
QUARTER_STRIDE = 256
SEG_WINDOW = 48
SUB_TILE = N_QUARTERS * SEG_WINDOW
LAST_SEG_WINDOW = QUARTER_STRIDE - SEG_WINDOW
SORT_ROWS = N_QUARTERS * QUARTER_STRIDE
POS_LANE = N_EXPERTS
assert QUARTER_SLOTS <= LAST_SEG_WINDOW and QUARTER_STRIDE % LANES == 0 and SEG_WINDOW % SEG_ALIGN == 0


def _expert_tile(hs_ref, cs_ref, ys_ref, wg_ref, wu_ref, wd_ref, g, starts):
    hs = jnp.concatenate([hs_ref[pl.ds(s, SEG_WINDOW), :] for s in starts], axis=0)
    cw = jnp.concatenate([cs_ref[pl.ds(s, SEG_WINDOW), :] for s in starts], axis=0)
    experts = range(g * EXPERTS_PER_GROUP, (g + 1) * EXPERTS_PER_GROUP)
    gate = jnp.concatenate([_dot(hs, wg_ref[e]) for e in experts], axis=1)
    up = jnp.concatenate([_dot(hs, wu_ref[e]) for e in experts], axis=1)
    scale = jnp.concatenate(
        [jnp.broadcast_to(cw[:, e:e + 1], (SUB_TILE, D_FF_EXPERT)) for e in experts], axis=1)
    hid = (_silu(gate) * up * scale).astype(BF16)
    y = _dot(hid, wd_ref[g])
    for q, s in enumerate(starts):
        ys_ref[pl.ds(s, SEG_WINDOW), :] = y[q * SEG_WINDOW:(q + 1) * SEG_WINDOW, :]


def _moe_kernel(xc_ref, xn_ref, modc_ref, modn_ref, wr_ref, rb_ref, before_ref, wg_ref, wu_ref, wd_ref, ln2_ref,
                o_ref, h_ref, z_ref, hs_ref, cs_ref, ys_ref):
    T = TOKEN_TILE

    def modulated_norm(x_ref, mod_ref):
        return (_layer_norm(x_ref[...]) * (1.0 + mod_ref[4:5, :]) + mod_ref[3:4, :]).astype(BF16)

    @pl.when(pl.program_id(0) == 0)
    def _():
        hs_ref[...] = jnp.zeros_like(hs_ref)
        cs_ref[...] = jnp.zeros_like(cs_ref)
        ys_ref[...] = jnp.zeros_like(ys_ref)
        z_ref[...] = jnp.zeros_like(z_ref)
        h_ref[...] = modulated_norm(xc_ref, modc_ref)

    h = h_ref[...]
    logits_t = _dot_nt(wr_ref[...], h)
    comb_rows, best = _route(logits_t, rb_ref[...])

    onehot = jnp.concatenate([b.astype(F32) for b in best] + [jnp.zeros((8 - N_EXPERT_GROUPS, T), F32)], axis=0)
    rank = _dot(onehot.astype(BF16), before_ref[...])
    counts, bases, pos_parts = [], [], []
    for q in range(N_QUARTERS):
        lanes = slice(q * QUARTER, (q + 1) * QUARTER)
        counts.append([jnp.sum(onehot[g:g + 1, lanes]).astype(jnp.int32) for g in range(N_EXPERT_GROUPS)])
        base_q = []
        acc = jnp.int32(0)
        for g in range(N_EXPERT_GROUPS):
            base_q.append(acc)
            acc = acc + ((counts[q][g] + (SEG_ALIGN - 1)) // SEG_ALIGN) * SEG_ALIGN
        bases.append(base_q)
        pos_parts.append(functools.reduce(
            lambda a, b: a + b,
            [onehot[g:g + 1, lanes] * (rank[g:g + 1, lanes] + base_q[g].astype(F32))
             for g in range(N_EXPERT_GROUPS)]))
    pos_f = jnp.concatenate(pos_parts, axis=1)

    tok_t = jnp.concatenate(comb_rows + [pos_f, jnp.zeros((LANES - N_EXPERTS - 1, T), F32)], axis=0)
    tok = jnp.transpose(tok_t)
    hi = tok.astype(BF16)
    r1 = tok - hi.astype(F32)
    mid = r1.astype(BF16)
    lo = (r1 - mid.astype(F32)).astype(BF16)
    pieces = jnp.concatenate([hi, mid, lo], axis=1)
    slot_iota = lax.broadcasted_iota(jnp.int32, (QUARTER_SLOTS, QUARTER), 0)
    for q in range(N_QUARTERS):
        rows = slice(q * QUARTER, (q + 1) * QUARTER)
        sort_mat = (slot_iota == pos_parts[q].astype(jnp.int32)).astype(BF16)
        dst = slice(q * QUARTER_STRIDE, q * QUARTER_STRIDE + QUARTER_SLOTS)
        hs_ref[dst, :] = _dot(sort_mat, h[rows, :]).astype(BF16)
        parts = _dot(sort_mat, pieces[rows, :])
        cs_ref[dst, :] = parts[:, :LANES] + parts[:, LANES:2 * LANES] + parts[:, 2 * LANES:]

    rows_per_group = T // N_EXPERT_GROUPS

    def window_starts(g, k):
        return [pl.multiple_of(q * QUARTER_STRIDE + jnp.minimum(bases[q][g] + k * SEG_WINDOW, LAST_SEG_WINDOW),
                               SEG_ALIGN) for q in range(N_QUARTERS)]

    for g in range(N_EXPERT_GROUPS):
        rows = slice(g * rows_per_group, (g + 1) * rows_per_group)
        h_ref[rows, :] = (_layer_norm(xn_ref[rows, :]) * (1.0 + modn_ref[4:5, :]) + modn_ref[3:4, :]).astype(BF16)
        o_ref[rows, :] = _layer_norm(z_ref[rows, :]) * ln2_ref[0:1, :] + ln2_ref[1:2, :]
        _expert_tile(hs_ref, cs_ref, ys_ref, wg_ref, wu_ref, wd_ref, g, window_starts(g, 0))

        def body(k, carry, g=g):
            _expert_tile(hs_ref, cs_ref, ys_ref, wg_ref, wu_ref, wd_ref, g, window_starts(g, k))
            return carry
        longest = functools.reduce(jnp.maximum, [counts[q][g] for q in range(N_QUARTERS)])
        lax.fori_loop(1, (longest + (SEG_WINDOW - 1)) // SEG_WINDOW, body, 0)

    lane_iota = lax.broadcasted_iota(jnp.int32, (QUARTER, QUARTER_STRIDE), 1)
    y_parts = []
    for q in range(N_QUARTERS):
        rows = slice(q * QUARTER, (q + 1) * QUARTER)
        unsort_mat = (tok[rows, POS_LANE:POS_LANE + 1].astype(jnp.int32) == lane_iota).astype(BF16)
        y_parts.append(_dot(unsort_mat, ys_ref[q * QUARTER_STRIDE:(q + 1) * QUARTER_STRIDE, :].astype(BF16)))
    y = jnp.concatenate(y_parts, axis=0)
    z_ref[...] = DEEPNORM_ALPHA * xc_ref[...] + (1.0 + modc_ref[5:6, :]) * y


def _moe(layer, x, mod, wr_t, rbias, wg, wu, wd, ln2):
    B, S, D = x.shape
    tile_spec, mod_spec, _, n_tiles = _tile_specs(x, layer)
    before = np.kron(np.eye(N_QUARTERS, dtype=np.float32), np.triu(np.ones((QUARTER, QUARTER), np.float32), k=1))
    before = jnp.asarray(before, dtype=BF16)

    return pl.pallas_call(
        _moe_kernel,
        grid=(n_tiles + 1,),
        in_specs=[
            tile_spec(0),
            tile_spec(1),
            mod_spec(0),
            mod_spec(1),
            _resident(wr_t.shape),
            _resident(rbias.shape),
            _resident(before.shape),
            _resident_layer(wg, layer),
            _resident_layer(wu, layer),
            _resident_layer(wd, layer),
            _resident_layer(ln2, layer),
        ],
        out_specs=tile_spec(-1),
        out_shape=jax.ShapeDtypeStruct((B, S, D), F32),
        scratch_shapes=[pltpu.VMEM((TOKEN_TILE, D_MODEL), BF16),
                        pltpu.VMEM((TOKEN_TILE, D_MODEL), F32),
                        pltpu.VMEM((SORT_ROWS, D_MODEL), BF16),
                        pltpu.VMEM((SORT_ROWS, LANES), F32),
                        pltpu.VMEM((SORT_ROWS, D_MODEL), F32)],
        compiler_params=pltpu.CompilerParams(
            dimension_semantics=("arbitrary",),
            vmem_limit_bytes=VMEM_LIMIT_BYTES),
        name="grouped_moe",
    )(x, x, mod, mod, wr_t, rbias, before, wg, wu, wd, ln2)


def kernel(x, c, rel_bias_table, w_router, router_bias, w_ada, b_ada, w_in, sinks, sg_w, sg_b, sg_ln_g, sg_ln_b,
           w_br_attn, w_br_sg, w_out, ln1_g, ln1_b, w_gate, w_up, w_down, ln2_g, ln2_b):
    L = w_in.shape[0]
    B = x.shape[0]
    mod = _ada_modulation(c, w_ada, b_ada).reshape(L, B, 6, D_MODEL)
    bias = _band_bias(rel_bias_table)
    bias = bias.reshape(N_KV_HEADS, Q_PER_KV * BLOCK, 2 * BLOCK)

    w_in_b = w_in.astype(BF16)
    w_bra_b = w_br_attn.astype(BF16)
    w_brs_b = w_br_sg.astype(BF16)
    w_out_b = w_out.astype(BF16)
    wg_b = w_gate.astype(BF16)
    wu_b = w_up.astype(BF16)
    wd_b = w_down.reshape(L, N_EXPERT_GROUPS, EXPERTS_PER_GROUP * D_FF_EXPERT, D_MODEL).astype(BF16)
    wr_t = jnp.transpose(w_router).astype(BF16)
    rbias = router_bias.reshape(N_EXPERTS, 1)
    sink_col = jnp.repeat(sinks.reshape(L, N_KV_HEADS, Q_PER_KV), BLOCK, axis=2).reshape(
        L, N_KV_HEADS, Q_PER_KV * BLOCK, 1)
    sgw_pair = jnp.concatenate([sg_w[:, 0::2], sg_w[:, 1::2]], axis=-1).astype(BF16)
    sgb_full = jnp.repeat(jnp.swapaxes(sg_b, 1, 2), SG_GROUP_DIM, axis=2)
    sgln = jnp.stack([sg_ln_g, sg_ln_b], axis=1)
    ln1 = jnp.stack([ln1_g, ln1_b], axis=1)
    ln2 = jnp.stack([ln2_g, ln2_b], axis=1)
    for l in range(L):
        x = _token_mix(l, x, mod, w_in_b, w_bra_b, w_brs_b, w_out_b, bias, sink_col, sgw_pair, sgb_full, sgln, ln1)
        x = _moe(l, x, mod, wr_t, rbias, wg_b, wu_b, wd_b, ln2)
    return x
```

```python
import functools
import math

import numpy as np
import jax
import jax.numpy as jnp
from jax import lax
from jax.experimental import pallas as pl
from jax.experimental.pallas import tpu as pltpu

D_MODEL = 1024
DEPTH = 4
N_Q_HEADS = 8
N_KV_HEADS = 2
HEAD_DIM = 64
Q_PER_KV = N_Q_HEADS // N_KV_HEADS
WINDOW = 128
BLOCK = WINDOW
ATTN_WIDTH = N_Q_HEADS * HEAD_DIM
KV_WIDTH = N_KV_HEADS * HEAD_DIM
N_BUCKETS = 32
MAX_DISTANCE = 128
N_SG_GROUPS = 8
SG_GROUP_DIM = 64
SG_CHUNK = 128
SG_WIDTH = N_SG_GROUPS * SG_GROUP_DIM
IN_WIDTH = ATTN_WIDTH + 2 * KV_WIDTH + 2 * SG_WIDTH + 2 * D_MODEL
N_EXPERTS = 16
N_EXPERT_GROUPS = 4
EXPERTS_PER_GROUP = N_EXPERTS // N_EXPERT_GROUPS
TOP_K = 2
D_FF_EXPERT = 256
DEEPNORM_ALPHA = (2 * DEPTH) ** 0.25
LN_EPS = 1e-5
NEG_INF = -1e30

OFF_Q = 0
OFF_KV = ATTN_WIDTH
OFF_SU = ATTN_WIDTH + 2 * KV_WIDTH
OFF_SV = OFF_SU + SG_WIDTH
OFF_GA = OFF_SV + SG_WIDTH
OFF_GS = OFF_GA + D_MODEL

LANES = 128
TOKEN_TILE = 512
BLOCKS_PER_TILE = TOKEN_TILE // BLOCK
REST_SLAB = 256
VMEM_LIMIT_BYTES = 56 * 1024 * 1024

BF16 = jnp.bfloat16
F32 = jnp.float32


def _layer_norm(x):
    mu = jnp.mean(x, axis=-1, keepdims=True)
    xc = x - mu
    var = jnp.mean(xc * xc, axis=-1, keepdims=True)
    return xc * lax.rsqrt(var + LN_EPS)


def _dot(a, b):
    return jnp.dot(a, b, preferred_element_type=F32)


def _dot_nt(a, b):
    return lax.dot_general(a, b, (((1,), (1,)), ((), ())), preferred_element_type=F32)


def _gelu(x):
    return 0.5 * x * (1.0 + lax.erf(x * (1.0 / math.sqrt(2.0))))


def _sigmoid(x):
    return 0.5 * (jnp.tanh(0.5 * x) + 1.0)


def _silu(x):
    return x * _sigmoid(x)


ADA_TILE = 1536


def _ada_kernel(c_ref, w_ref, b_ref, o_ref):
    cond = _silu(c_ref[...]).astype(BF16)
    o_ref[...] = _dot(cond, w_ref[...].astype(BF16)) + b_ref[...]


def _ada_modulation(c, w_ada, b_ada):
    L, D, N = w_ada.shape
    B = c.shape[0]
    return pl.pallas_call(
        _ada_kernel,
        grid=(L, N // ADA_TILE),
        in_specs=[
            pl.BlockSpec((B, D), lambda l, n: (0, 0)),
            pl.BlockSpec((None, D, ADA_TILE), lambda l, n: (l, 0, n)),
            pl.BlockSpec((None, 1, ADA_TILE), lambda l, n: (l, 0, n)),
        ],
        out_specs=pl.BlockSpec((None, B, ADA_TILE), lambda l, n: (l, 0, n)),
        out_shape=jax.ShapeDtypeStruct((L, B, N), F32),
        compiler_params=pltpu.CompilerParams(
            dimension_semantics=("arbitrary", "arbitrary"),
            vmem_limit_bytes=VMEM_LIMIT_BYTES),
        name="ada_modulation",
    )(c, w_ada, b_ada.reshape(L, 1, N))


def _band_buckets():
    qi = np.arange(BLOCK)[:, None]
    kj = np.arange(2 * BLOCK)[None, :]
    rel = qi + BLOCK - kj
    n = np.maximum(rel, 0)
    max_exact = N_BUCKETS // 2
    nf = np.maximum(n, 1).astype(np.float32)
    large = max_exact + (np.log(nf / np.float32(max_exact)) / np.float32(math.log(MAX_DISTANCE / max_exact))
                         * np.float32(N_BUCKETS - max_exact)).astype(np.int32)
    large = np.minimum(large, N_BUCKETS - 1)
    bucket = np.where(n < max_exact, n, large).astype(np.int32)
    in_window = ((rel >= 0) & (rel < WINDOW)).astype(np.int32)
    return bucket, in_window


def _bias_kernel(table_ref, bucket_ref, window_ref, o_ref):
    bucket = bucket_ref[...]
    in_window = window_ref[...] > 0
    for head in range(N_Q_HEADS):
        acc = jnp.zeros((BLOCK, 2 * BLOCK), F32)
        for b in range(N_BUCKETS):
            acc = jnp.where(bucket == b, table_ref[b, head], acc)
        o_ref[head] = jnp.where(in_window, acc, NEG_INF)


def _band_bias(rel_bias_table):
    bucket, in_window = _band_buckets()
    return pl.pallas_call(
        _bias_kernel,
        in_specs=[
            pl.BlockSpec(memory_space=pltpu.SMEM),
            pl.BlockSpec(memory_space=pltpu.VMEM),
            pl.BlockSpec(memory_space=pltpu.VMEM),
        ],
        out_specs=pl.BlockSpec(memory_space=pltpu.VMEM),
        out_shape=jax.ShapeDtypeStruct((N_Q_HEADS, BLOCK, 2 * BLOCK), F32),
        name="band_bias",
    )(rel_bias_table, jnp.asarray(bucket), jnp.asarray(in_window))


ANCHOR_ROWS = 16


def _token_mix_kernel(xc_ref, xn_ref, modc_ref, modn_ref, w_in_ref, w_bra_ref, w_brs_ref, w_out_ref, bias_ref,
                      sink_ref, sgw_ref, sgb_ref, sgln_ref, ln1_ref, o_ref, h_ref, z_ref, kprev_ref, vprev_ref,
                      *, tiles_per_seq, n_tiles):
    T = TOKEN_TILE
    step = pl.program_id(0)
    j = jnp.minimum(step, n_tiles - 1) % tiles_per_seq
    cur = pl.multiple_of((step % 2) * T, T)
    nxt = pl.multiple_of(((step + 1) % 2) * T, T)

    def modulated_norm(x, mod_ref):
        return (_layer_norm(x) * (1.0 + mod_ref[1:2, :]) + mod_ref[0:1, :]).astype(BF16)

    @pl.when(step == 0)
    def _():
        z_ref[...] = jnp.zeros_like(z_ref)
        h_ref[...] = jnp.zeros_like(h_ref)
        h_ref[0:T, :] = modulated_norm(xc_ref[...], modc_ref)

    @pl.when(j == 0)
    def _():
        kprev_ref[...] = jnp.zeros_like(kprev_ref)
        vprev_ref[...] = jnp.zeros_like(vprev_ref)

    def h_tile():
        return h_ref[pl.ds(cur, T), :]

    side_rows = T // 4

    def neighbour_norms(c):
        rows = slice(c * side_rows, (c + 1) * side_rows)
        h_ref[pl.ds(nxt + c * side_rows, side_rows), :] = modulated_norm(xn_ref[rows, :], modn_ref)
        out = _layer_norm(z_ref[rows, :]) * ln1_ref[0:1, :] + ln1_ref[1:2, :]
        o_ref[rows, :] = out
        fold = functools.reduce(lambda a, b: a + b,
                                [out[r:r + ANCHOR_ROWS, :] for r in range(0, side_rows, ANCHOR_ROWS)])
        h_ref[2 * T:2 * T + ANCHOR_ROWS, :] = fold.astype(BF16)

    lane = lax.broadcasted_iota(jnp.int32, (1, LANES), 1)
    lo = lane < HEAD_DIM

    q = (_dot(h_tile(), w_in_ref[:, OFF_Q:OFF_Q + ATTN_WIDTH]) * (HEAD_DIM ** -0.5)).astype(BF16)
    kv = _dot(h_tile(), w_in_ref[:, OFF_KV:OFF_KV + 2 * KV_WIDTH]).astype(BF16)
    k_full = jnp.concatenate([kprev_ref[...], kv[:, :KV_WIDTH]], axis=0)
    v_full = jnp.concatenate([vprev_ref[...], kv[:, KV_WIDTH:]], axis=0)
    kprev_ref[...] = kv[TOKEN_TILE - BLOCK:, :KV_WIDTH]
    vprev_ref[...] = kv[TOKEN_TILE - BLOCK:, KV_WIDTH:]

    k_sw = pltpu.roll(k_full, HEAD_DIM, 1)
    k_dup = (jnp.where(lo, k_full, k_sw), jnp.where(lo, k_sw, k_full))
    one = jnp.ones((), BF16)
    v_ext = (jnp.where(lo, v_full, one), jnp.where(lo, one, v_full))

    n_rest = (IN_WIDTH - OFF_SU) // REST_SLAB
    rest = []

    def project_rest(count):
        for _ in range(count):
            if len(rest) % 3 == 2:
                neighbour_norms(len(rest) // 3)
            c0 = OFF_SU + len(rest) * REST_SLAB
            rest.append(_dot(h_tile(), w_in_ref[:, c0:c0 + REST_SLAB]))

    key_in_prev = lax.broadcasted_iota(jnp.int32, (1, 2 * BLOCK), 1) < BLOCK
    zero = jnp.zeros((), BF16)
    units = [(n, hk) for n in range(BLOCKS_PER_TILE) for hk in range(N_KV_HEADS)]

    def scores(n, hk):
        r0 = n * BLOCK
        qa = q[r0:r0 + BLOCK, hk * 256:hk * 256 + LANES]
        qb = q[r0:r0 + BLOCK, hk * 256 + LANES:hk * 256 + 2 * LANES]
        qs = jnp.concatenate([jnp.where(lo, qa, zero), jnp.where(lo, zero, qa),
                              jnp.where(lo, qb, zero), jnp.where(lo, zero, qb)], axis=0)
        s = _dot_nt(qs, k_dup[hk][r0:r0 + 2 * BLOCK]) + bias_ref[hk]
        if n == 0:
            s = jnp.where(jnp.logical_and(j == 0, key_in_prev), NEG_INF, s)
        return s

    def attend(s, n, hk):
        r0 = n * BLOCK
        sink = sink_ref[hk]
        m = jnp.maximum(jnp.max(s, axis=-1, keepdims=True), sink)
        p = jnp.exp(s - m).astype(BF16)
        pv = _dot(p, v_ext[hk][r0:r0 + 2 * BLOCK])
        pv_sw = pltpu.roll(pv, HEAD_DIM, 1)
        row_sum = jnp.where(lo, pv_sw, pv) if hk == 0 else jnp.where(lo, pv, pv_sw)
        out = pv / (row_sum + jnp.exp(sink - m))
        out_sw = pltpu.roll(out, HEAD_DIM, 1)
        slabs = []
        for pair in range(2):
            first = slice(2 * pair * BLOCK, (2 * pair + 1) * BLOCK)
            second = slice((2 * pair + 1) * BLOCK, (2 * pair + 2) * BLOCK)
            if hk == 0:
                slabs.append(jnp.where(lo, out[first], out_sw[second]))
            else:
                slabs.append(jnp.where(lo, out_sw[first], out[second]))
        return slabs

    n_sg_slabs = 2 * SG_WIDTH // REST_SLAB
    project_rest(n_sg_slabs)
    su = jnp.concatenate(rest[:n_sg_slabs // 2], axis=1)
    sv = jnp.concatenate(rest[n_sg_slabs // 2:n_sg_slabs], axis=1)
    u = _gelu(su)
    vn = (_layer_norm(_gelu(sv)) * sgln_ref[0:1, :] + sgln_ref[1:2, :]).astype(BF16)
    row = lax.broadcasted_iota(jnp.int32, (SG_CHUNK, 2 * SG_CHUNK), 0)
    col = lax.broadcasted_iota(jnp.int32, (SG_CHUNK, 2 * SG_CHUNK), 1)
    causal = (col & (SG_CHUNK - 1)) <= row
    mix_jobs = [(s_idx, c) for s_idx in range(SG_WIDTH // LANES) for c in range(TOKEN_TILE // SG_CHUNK)]
    w_pairs = {}
    mixed_blocks = {}

    def mix(count):
        for _ in range(count):
            s_idx, c = mix_jobs[len(mixed_blocks)]
            if s_idx not in w_pairs:
                w_pairs[s_idx] = jnp.where(causal, sgw_ref[s_idx], zero)
            slab = vn[c * SG_CHUNK:(c + 1) * SG_CHUNK, s_idx * LANES:(s_idx + 1) * LANES]
            rhs = jnp.concatenate([jnp.where(lo, slab, zero), jnp.where(lo, zero, slab)], axis=0)
            mixed_blocks[(s_idx, c)] = _dot(w_pairs[s_idx], rhs) + sgb_ref[:, s_idx * LANES:(s_idx + 1) * LANES]

    slabs = {}
    s_next = scores(*units[0])
    for i, (n, hk) in enumerate(units):
        s_cur = s_next
        project_rest(1)
        if i + 1 < len(units):
            s_next = scores(*units[i + 1])
        mix(len(mix_jobs) // len(units))
        slabs[(n, hk)] = attend(s_cur, n, hk)
    mix(len(mix_jobs) - len(mixed_blocks))
    mixed = jnp.concatenate(
        [jnp.concatenate([mixed_blocks[(s_idx, c)] for c in range(TOKEN_TILE // SG_CHUNK)], axis=0)
         for s_idx in range(SG_WIDTH // LANES)], axis=1)
    sg = (u * mixed).astype(BF16)
    s_br = _dot(sg, w_brs_ref[...])
    attn = jnp.concatenate(
        [jnp.concatenate(slabs[(n, 0)] + slabs[(n, 1)], axis=1).astype(BF16) for n in range(BLOCKS_PER_TILE)],
        axis=0)
    a_br = _dot(attn, w_bra_ref[...])
    project_rest(n_rest - len(rest))
    ga = jnp.concatenate(rest[n_sg_slabs:n_sg_slabs + D_MODEL // REST_SLAB], axis=1)
    gs = jnp.concatenate(rest[n_sg_slabs + D_MODEL // REST_SLAB:], axis=1)

    merged = (_sigmoid(ga) * a_br + _sigmoid(gs) * s_br).astype(BF16)
    y = _dot(merged, w_out_ref[...])
    z_ref[...] = DEEPNORM_ALPHA * xc_ref[...] + (1.0 + modc_ref[2:3, :]) * y


def _resident(shape):
    ndim = len(shape)
    return pl.BlockSpec(shape, lambda *_: (0,) * ndim, pipeline_mode=pl.Buffered(1))


def _resident_layer(stacked, layer):
    shape = stacked.shape[1:]
    ndim = len(shape)
    return pl.BlockSpec((None,) + shape, lambda *_: (layer,) + (0,) * ndim, pipeline_mode=pl.Buffered(1))


def _tile_specs(x, layer):
    B, S, D = x.shape
    tiles_per_seq = S // TOKEN_TILE
    n_tiles = B * tiles_per_seq

    def tile_spec(offset):
        def index(s):
            t = jnp.clip(s + offset, 0, n_tiles - 1)
            return (t // tiles_per_seq, t % tiles_per_seq, 0)
        return pl.BlockSpec((None, TOKEN_TILE, D), index)

    def mod_spec(offset):
        def index(s):
            t = jnp.clip(s + offset, 0, n_tiles - 1)
            return (layer, t // tiles_per_seq, 0, 0)
        return pl.BlockSpec((None, None, 6, D_MODEL), index)

    return tile_spec, mod_spec, tiles_per_seq, n_tiles


def _token_mix(layer, x, mod, w_in, w_bra, w_brs, w_out, bias, sink_col, sgw_pair, sgb_full, sgln, ln1):
    B, S, D = x.shape
    tile_spec, mod_spec, tiles_per_seq, n_tiles = _tile_specs(x, layer)
    return pl.pallas_call(
        functools.partial(_token_mix_kernel, tiles_per_seq=tiles_per_seq, n_tiles=n_tiles),
        grid=(n_tiles + 1,),
        in_specs=[
            tile_spec(0),
            tile_spec(1),
            mod_spec(0),
            mod_spec(1),
            _resident_layer(w_in, layer),
            _resident_layer(w_bra, layer),
            _resident_layer(w_brs, layer),
            _resident_layer(w_out, layer),
            _resident(bias.shape),
            _resident_layer(sink_col, layer),
            _resident_layer(sgw_pair, layer),
            _resident_layer(sgb_full, layer),
            _resident_layer(sgln, layer),
            _resident_layer(ln1, layer),
        ],
        out_specs=tile_spec(-1),
        out_shape=jax.ShapeDtypeStruct((B, S, D), F32),
        scratch_shapes=[pltpu.VMEM((2 * TOKEN_TILE + ANCHOR_ROWS, D_MODEL), BF16),
                        pltpu.VMEM((TOKEN_TILE, D_MODEL), F32),
                        pltpu.VMEM((BLOCK, KV_WIDTH), BF16), pltpu.VMEM((BLOCK, KV_WIDTH), BF16)],
        compiler_params=pltpu.CompilerParams(
            dimension_semantics=("arbitrary",),
            vmem_limit_bytes=VMEM_LIMIT_BYTES),
        name="token_mix",
    )(x, x, mod, mod, w_in, w_bra, w_brs, w_out, bias, sink_col, sgw_pair, sgb_full, sgln, ln1)


def _route(rows, rbias):
    m = functools.reduce(jnp.maximum, rows)
    ex = [jnp.exp(r - m) for r in rows]
    denom = functools.reduce(lambda a, b: a + b, ex)
    probs = [e_ / denom for e_ in ex]
    sel = [probs[e] + rbias[e:e + 1, :] for e in range(N_EXPERTS)]

    def beats(a, b, a_first):
        return (a >= b) if a_first else (a > b)

    chosen = []
    score = []
    for g in range(N_EXPERT_GROUPS):
        members = list(range(g * EXPERTS_PER_GROUP, (g + 1) * EXPERTS_PER_GROUP))
        total = None
        for i in members:
            rank = None
            for o in members:
                if o == i:
                    continue
                ahead = beats(sel[o], sel[i], o < i).astype(F32)
                rank = ahead if rank is None else rank + ahead
            pick = rank < TOP_K
            chosen.append(pick)
            term = jnp.where(pick, sel[i], 0.0)
            total = term if total is None else total + term
        score.append(total)
    best = []
    for g in range(N_EXPERT_GROUPS):
        is_best = None
        for o in range(N_EXPERT_GROUPS):
            if o == g:
                continue
            wins = beats(score[g], score[o], g < o)
            is_best = wins if is_best is None else jnp.logical_and(is_best, wins)
        best.append(is_best)
    picked = [jnp.where(jnp.logical_and(chosen[e], best[e // EXPERTS_PER_GROUP]), probs[e], 0.0)
              for e in range(N_EXPERTS)]
    norm = functools.reduce(lambda a, b: a + b, picked)
    return [p_ / norm for p_ in picked], best


N_QUARTERS = 4
QUARTER = TOKEN_TILE // N_QUARTERS
SEG_ALIGN = 16
QUARTER_SLOTS = QUARTER + N_EXPERT_GROUPS * SEG_ALIGN
QUARTER_STRIDE = 256
SEG_WINDOW = 48
SUB_TILE = N_QUARTERS * SEG_WINDOW
LAST_SEG_WINDOW = QUARTER_STRIDE - SEG_WINDOW
SORT_ROWS = N_QUARTERS * QUARTER_STRIDE
POS_LANE = N_EXPERTS
assert QUARTER_SLOTS <= LAST_SEG_WINDOW and QUARTER_STRIDE % LANES == 0 and SEG_WINDOW % SEG_ALIGN == 0


def _expert_tile(hs_ref, cs_ref, ys_ref, wg_ref, wu_ref, wd_ref, g, starts):
    hs = jnp.concatenate([hs_ref[pl.ds(s, SEG_WINDOW), :] for s in starts], axis=0)
    cw = jnp.concatenate([cs_ref[pl.ds(s, SEG_WINDOW), :] for s in starts], axis=0)
    experts = range(g * EXPERTS_PER_GROUP, (g + 1) * EXPERTS_PER_GROUP)
    gate = jnp.concatenate([_dot(hs, wg_ref[e]) for e in experts], axis=1)
    up = jnp.concatenate([_dot(hs, wu_ref[e]) for e in experts], axis=1)
    scale = jnp.concatenate(
        [jnp.broadcast_to(cw[:, e:e + 1], (SUB_TILE, D_FF_EXPERT)) for e in experts], axis=1)
    hid = (_silu(gate) * up * scale).astype(BF16)
    y = _dot(hid, wd_ref[g]).astype(BF16)
    for q, s in enumerate(starts):
        ys_ref[pl.ds(s, SEG_WINDOW), :] = y[q * SEG_WINDOW:(q + 1) * SEG_WINDOW, :]


def _moe_kernel(xc_ref, xn_ref, modc_ref, modn_ref, wr_ref, rb_ref, before_ref, wg_ref, wu_ref, wd_ref, ln2_ref,
                o_ref, h_ref, z_ref, hs_ref, cs_ref, ys_ref):
    T = TOKEN_TILE

    def modulated_norm(x_ref, mod_ref):
        return (_layer_norm(x_ref[...]) * (1.0 + mod_ref[4:5, :]) + mod_ref[3:4, :]).astype(BF16)

    @pl.when(pl.program_id(0) == 0)
    def _():
        hs_ref[...] = jnp.zeros_like(hs_ref)
        cs_ref[...] = jnp.zeros_like(cs_ref)
        ys_ref[...] = jnp.zeros_like(ys_ref)
        z_ref[...] = jnp.zeros_like(z_ref)
        h_ref[...] = modulated_norm(xc_ref, modc_ref)

    h = h_ref[...]
    logits_t = _dot_nt(wr_ref[...], h)
    quarter_lanes = [slice(q * QUARTER, (q + 1) * QUARTER) for q in range(N_QUARTERS)]
    comb, best = _route([jnp.concatenate([logits_t[e:e + 1, ql] for ql in quarter_lanes], axis=0)
                         for e in range(N_EXPERTS)], rb_ref[...])

    counts, bases, slot_cols = [], [], []
    slot_iota = lax.broadcasted_iota(jnp.int32, (QUARTER_SLOTS, QUARTER), 0)
    for q in range(N_QUARTERS):
        onehot = jnp.concatenate([best[g][q:q + 1, :].astype(F32) for g in range(N_EXPERT_GROUPS)]
                                 + [jnp.zeros((8 - N_EXPERT_GROUPS, QUARTER), F32)], axis=0)
        rank = _dot(onehot.astype(BF16), before_ref[...])
        counts.append([jnp.sum(onehot[g:g + 1, :]).astype(jnp.int32) for g in range(N_EXPERT_GROUPS)])
        base_q = []
        acc = jnp.int32(0)
        for g in range(N_EXPERT_GROUPS):
            base_q.append(acc)
            acc = acc + ((counts[q][g] + (SEG_ALIGN - 1)) // SEG_ALIGN) * SEG_ALIGN
        bases.append(base_q)
        pos_f = functools.reduce(
            lambda a, b: a + b,
            [onehot[g:g + 1, :] * (rank[g:g + 1, :] + base_q[g].astype(F32))
             for g in range(N_EXPERT_GROUPS)])

        tok_t = jnp.concatenate([comb[e][q:q + 1, :] for e in range(N_EXPERTS)]
                                + [pos_f, jnp.zeros((LANES - N_EXPERTS - 1, QUARTER), F32)], axis=0)
        tok = jnp.transpose(tok_t)
        slot_cols.append(tok[:, POS_LANE:POS_LANE + 1].astype(jnp.int32))
        hi = tok.astype(BF16)
        r1 = tok - hi.astype(F32)
        mid = r1.astype(BF16)
        lo = (r1 - mid.astype(F32)).astype(BF16)
        sort_mat = (slot_iota == pos_f.astype(jnp.int32)).astype(BF16)
        dst = slice(q * QUARTER_STRIDE, q * QUARTER_STRIDE + QUARTER_SLOTS)
        hs_ref[dst, :] = _dot(sort_mat, h[q * QUARTER:(q + 1) * QUARTER, :]).astype(BF16)
        parts = _dot(sort_mat, jnp.concatenate([hi, mid, lo], axis=1))
        cs_ref[dst, :] = parts[:, :LANES] + parts[:, LANES:2 * LANES] + parts[:, 2 * LANES:]

    rows_per_group = T // N_EXPERT_GROUPS

    def window_starts(g, k):
        return [pl.multiple_of(q * QUARTER_STRIDE + jnp.minimum(bases[q][g] + k * SEG_WINDOW, LAST_SEG_WINDOW),
                               SEG_ALIGN) for q in range(N_QUARTERS)]

    for g in range(N_EXPERT_GROUPS):
        rows = slice(g * rows_per_group, (g + 1) * rows_per_group)
        h_ref[rows, :] = (_layer_norm(xn_ref[rows, :]) * (1.0 + modn_ref[4:5, :]) + modn_ref[3:4, :]).astype(BF16)
        o_ref[rows, :] = _layer_norm(z_ref[rows, :]) * ln2_ref[0:1, :] + ln2_ref[1:2, :]
        _expert_tile(hs_ref, cs_ref, ys_ref, wg_ref, wu_ref, wd_ref, g, window_starts(g, 0))

        def body(k, carry, g=g):
            _expert_tile(hs_ref, cs_ref, ys_ref, wg_ref, wu_ref, wd_ref, g, window_starts(g, k))
            return carry
        longest = functools.reduce(jnp.maximum, [counts[q][g] for q in range(N_QUARTERS)])
        lax.fori_loop(1, (longest + (SEG_WINDOW - 1)) // SEG_WINDOW, body, 0)

    lane_iota = lax.broadcasted_iota(jnp.int32, (QUARTER, QUARTER_STRIDE), 1)
    y_parts = []
    for q in range(N_QUARTERS):
        unsort_mat = (slot_cols[q] == lane_iota).astype(BF16)
        y_parts.append(_dot(unsort_mat, ys_ref[q * QUARTER_STRIDE:(q + 1) * QUARTER_STRIDE, :]))
    y = jnp.concatenate(y_parts, axis=0)
    z_ref[...] = DEEPNORM_ALPHA * xc_ref[...] + (1.0 + modc_ref[5:6, :]) * y


def _moe(layer, x, mod, wr_t, rbias, wg, wu, wd, ln2):
    B, S, D = x.shape
    tile_spec, mod_spec, _, n_tiles = _tile_specs(x, layer)
    before = jnp.asarray(np.triu(np.ones((QUARTER, QUARTER), np.float32), k=1), dtype=BF16)

    return pl.pallas_call(
        _moe_kernel,
        grid=(n_tiles + 1,),
        in_specs=[
            tile_spec(0),
            tile_spec(1),
            mod_spec(0),
            mod_spec(1),
            _resident(wr_t.shape),
            _resident(rbias.shape),
            _resident(before.shape),
            _resident_layer(wg, layer),
            _resident_layer(wu, layer),
            _resident_layer(wd, layer),
            _resident_layer(ln2, layer),
        ],
        out_specs=tile_spec(-1),
        out_shape=jax.ShapeDtypeStruct((B, S, D), F32),
        scratch_shapes=[pltpu.VMEM((TOKEN_TILE, D_MODEL), BF16),
                        pltpu.VMEM((TOKEN_TILE, D_MODEL), F32),
                        pltpu.VMEM((SORT_ROWS, D_MODEL), BF16),
                        pltpu.VMEM((SORT_ROWS, LANES), F32),
                        pltpu.VMEM((SORT_ROWS, D_MODEL), BF16)],
        compiler_params=pltpu.CompilerParams(
            dimension_semantics=("arbitrary",),
            vmem_limit_bytes=VMEM_LIMIT_BYTES),
        name="grouped_moe",
    )(x, x, mod, mod, wr_t, rbias, before, wg, wu, wd, ln2)


def kernel(x, c, rel_bias_table, w_router, router_bias, w_ada, b_ada, w_in, sinks, sg_w, sg_b, sg_ln_g, sg_ln_b,
           w_br_attn, w_br_sg, w_out, ln1_g, ln1_b, w_gate, w_up, w_down, ln2_g, ln2_b):
    L = w_in.shape[0]
    B = x.shape[0]
    mod = _ada_modulation(c, w_ada, b_ada).reshape(L, B, 6, D_MODEL)
    bias = _band_bias(rel_bias_table)
    bias = bias.reshape(N_KV_HEADS, Q_PER_KV * BLOCK, 2 * BLOCK)

    w_in_b = w_in.astype(BF16)
    w_bra_b = w_br_attn.astype(BF16)
    w_brs_b = w_br_sg.astype(BF16)
    w_out_b = w_out.astype(BF16)
    wg_b = w_gate.astype(BF16)
    wu_b = w_up.astype(BF16)
    wd_b = w_down.reshape(L, N_EXPERT_GROUPS, EXPERTS_PER_GROUP * D_FF_EXPERT, D_MODEL).astype(BF16)
    wr_t = jnp.transpose(w_router).astype(BF16)
    rbias = router_bias.reshape(N_EXPERTS, 1)
    sink_col = jnp.repeat(sinks.reshape(L, N_KV_HEADS, Q_PER_KV), BLOCK, axis=2).reshape(
        L, N_KV_HEADS, Q_PER_KV * BLOCK, 1)
    sgw_pair = jnp.concatenate([sg_w[:, 0::2], sg_w[:, 1::2]], axis=-1).astype(BF16)
    sgb_full = jnp.repeat(jnp.swapaxes(sg_b, 1, 2), SG_GROUP_DIM, axis=2)
    sgln = jnp.stack([sg_ln_g, sg_ln_b], axis=1)
    ln1 = jnp.stack([ln1_g, ln1_b], axis=1)
    ln2 = jnp.stack([ln2_g, ln2_b], axis=1)
    for l in range(L):
        x = _token_mix(l, x, mod, w_in_b, w_bra_b, w_brs_b, w_out_b, bias, sink_col, sgw_pair, sgb_full, sgln, ln1)
        x = _moe(l, x, mod, wr_t, rbias, wg_b, wu_b, wd_b, ln2)
    return x
```

```python
import functools
import math

import numpy as np
import jax
import jax.numpy as jnp
from jax import lax
from jax.experimental import pallas as pl
from jax.experimental.pallas import tpu as pltpu

D_MODEL = 1024
DEPTH = 4
N_Q_HEADS = 8
N_KV_HEADS = 2
HEAD_DIM = 64
Q_PER_KV = N_Q_HEADS // N_KV_HEADS
WINDOW = 128
BLOCK = WINDOW
ATTN_WIDTH = N_Q_HEADS * HEAD_DIM
KV_WIDTH = N_KV_HEADS * HEAD_DIM
N_BUCKETS = 32
MAX_DISTANCE = 128
N_SG_GROUPS = 8
SG_GROUP_DIM = 64
SG_CHUNK = 128
SG_WIDTH = N_SG_GROUPS * SG_GROUP_DIM
IN_WIDTH = ATTN_WIDTH + 2 * KV_WIDTH + 2 * SG_WIDTH + 2 * D_MODEL
N_EXPERTS = 16
N_EXPERT_GROUPS = 4
EXPERTS_PER_GROUP = N_EXPERTS // N_EXPERT_GROUPS
TOP_K = 2
D_FF_EXPERT = 256
DEEPNORM_ALPHA = (2 * DEPTH) ** 0.25
LN_EPS = 1e-5
NEG_INF = -1e30

OFF_Q = 0
OFF_KV = ATTN_WIDTH
OFF_SU = ATTN_WIDTH + 2 * KV_WIDTH
OFF_SV = OFF_SU + SG_WIDTH
OFF_GA = OFF_SV + SG_WIDTH
OFF_GS = OFF_GA + D_MODEL

LANES = 128
TOKEN_TILE = 512
BLOCKS_PER_TILE = TOKEN_TILE // BLOCK
REST_SLAB = 256
VMEM_LIMIT_BYTES = 56 * 1024 * 1024

BF16 = jnp.bfloat16
F32 = jnp.float32


def _layer_norm(x):
    mu = jnp.mean(x, axis=-1, keepdims=True)
    xc = x - mu
    var = jnp.mean(xc * xc, axis=-1, keepdims=True)
    return xc * lax.rsqrt(var + LN_EPS)


def _dot(a, b):
    return jnp.dot(a, b, preferred_element_type=F32)


def _dot_nt(a, b):
    return lax.dot_general(a, b, (((1,), (1,)), ((), ())), preferred_element_type=F32)


def _gelu(x):
    return 0.5 * x * (1.0 + lax.erf(x * (1.0 / math.sqrt(2.0))))


def _sigmoid(x):
    return 0.5 * (jnp.tanh(0.5 * x) + 1.0)


def _silu(x):
    return x * _sigmoid(x)


ADA_TILE = 1536


def _ada_kernel(c_ref, w_ref, b_ref, o_ref):
    cond = _silu(c_ref[...]).astype(BF16)
    o_ref[...] = _dot(cond, w_ref[...].astype(BF16)) + b_ref[...]


def _ada_modulation(c, w_ada, b_ada):
    L, D, N = w_ada.shape
    B = c.shape[0]
    return pl.pallas_call(
        _ada_kernel,
        grid=(L, N // ADA_TILE),
        in_specs=[
            pl.BlockSpec((B, D), lambda l, n: (0, 0)),
            pl.BlockSpec((None, D, ADA_TILE), lambda l, n: (l, 0, n)),
            pl.BlockSpec((None, 1, ADA_TILE), lambda l, n: (l, 0, n)),
        ],
        out_specs=pl.BlockSpec((None, B, ADA_TILE), lambda l, n: (l, 0, n)),
        out_shape=jax.ShapeDtypeStruct((L, B, N), F32),
        compiler_params=pltpu.CompilerParams(
            dimension_semantics=("arbitrary", "arbitrary"),
            vmem_limit_bytes=VMEM_LIMIT_BYTES),
        name="ada_modulation",
    )(c, w_ada, b_ada.reshape(L, 1, N))


def _band_buckets():
    qi = np.arange(BLOCK)[:, None]
    kj = np.arange(2 * BLOCK)[None, :]
    rel = qi + BLOCK - kj
    n = np.maximum(rel, 0)
    max_exact = N_BUCKETS // 2
    nf = np.maximum(n, 1).astype(np.float32)
    large = max_exact + (np.log(nf / np.float32(max_exact)) / np.float32(math.log(MAX_DISTANCE / max_exact))
                         * np.float32(N_BUCKETS - max_exact)).astype(np.int32)
    large = np.minimum(large, N_BUCKETS - 1)
    bucket = np.where(n < max_exact, n, large).astype(np.int32)
    in_window = ((rel >= 0) & (rel < WINDOW)).astype(np.int32)
    return bucket, in_window


def _bias_kernel(table_ref, bucket_ref, window_ref, o_ref):
    bucket = bucket_ref[...]
    in_window = window_ref[...] > 0
    for head in range(N_Q_HEADS):
        acc = jnp.zeros((BLOCK, 2 * BLOCK), F32)
        for b in range(N_BUCKETS):
            acc = jnp.where(bucket == b, table_ref[b, head], acc)
        o_ref[head] = jnp.where(in_window, acc, NEG_INF)


def _band_bias(rel_bias_table):
    bucket, in_window = _band_buckets()
    return pl.pallas_call(
        _bias_kernel,
        in_specs=[
            pl.BlockSpec(memory_space=pltpu.SMEM),
            pl.BlockSpec(memory_space=pltpu.VMEM),
            pl.BlockSpec(memory_space=pltpu.VMEM),
        ],
        out_specs=pl.BlockSpec(memory_space=pltpu.VMEM),
        out_shape=jax.ShapeDtypeStruct((N_Q_HEADS, BLOCK, 2 * BLOCK), F32),
        name="band_bias",
    )(rel_bias_table, jnp.asarray(bucket), jnp.asarray(in_window))


ANCHOR_ROWS = 16


def _mix_input_norm(x, mod_ref):
    return (_layer_norm(x) * (1.0 + mod_ref[1:2, :]) + mod_ref[0:1, :]).astype(BF16)


def _token_mix_kernel(xc_ref, xn_ref, modc_ref, modn_ref, w_in_ref, w_bra_ref, w_brs_ref, w_out_ref, bias_ref,
                      sink_ref, sgw_ref, sgb_ref, sgln_ref, ln1_ref, o_ref, h_ref, z_ref, kprev_ref, vprev_ref,
                      *, tiles_per_seq, n_tiles):
    step = pl.program_id(0)

    @pl.when(step == 0)
    def _():
        z_ref[...] = jnp.zeros_like(z_ref)
        h_ref[...] = jnp.zeros_like(h_ref)
        h_ref[0:TOKEN_TILE, :] = _mix_input_norm(xc_ref[...], modc_ref)

    @pl.when(step % tiles_per_seq == 0)
    def _():
        kprev_ref[...] = jnp.zeros_like(kprev_ref)
        vprev_ref[...] = jnp.zeros_like(vprev_ref)

    pl.when(step < n_tiles)(functools.partial(
        _token_mix_tile, xc_ref, xn_ref, modc_ref, modn_ref, w_in_ref, w_bra_ref, w_brs_ref, w_out_ref, bias_ref,
        sink_ref, sgw_ref, sgb_ref, sgln_ref, ln1_ref, o_ref, h_ref, z_ref, kprev_ref, vprev_ref,
        tiles_per_seq=tiles_per_seq))

    @pl.when(step == n_tiles)
    def _():
        o_ref[...] = _layer_norm(z_ref[...]) * ln1_ref[0:1, :] + ln1_ref[1:2, :]


def _token_mix_tile(xc_ref, xn_ref, modc_ref, modn_ref, w_in_ref, w_bra_ref, w_brs_ref, w_out_ref, bias_ref,
                    sink_ref, sgw_ref, sgb_ref, sgln_ref, ln1_ref, o_ref, h_ref, z_ref, kprev_ref, vprev_ref,
                    *, tiles_per_seq):
    T = TOKEN_TILE
    step = pl.program_id(0)
    j = step % tiles_per_seq
    cur = pl.multiple_of((step % 2) * T, T)
    nxt = pl.multiple_of(((step + 1) % 2) * T, T)
    modulated_norm = _mix_input_norm

    def h_tile():
        return h_ref[pl.ds(cur, T), :]

    side_rows = T // 4

    def neighbour_norms(c):
        rows = slice(c * side_rows, (c + 1) * side_rows)
        h_ref[pl.ds(nxt + c * side_rows, side_rows), :] = modulated_norm(xn_ref[rows, :], modn_ref)
        out = _layer_norm(z_ref[rows, :]) * ln1_ref[0:1, :] + ln1_ref[1:2, :]
        o_ref[rows, :] = out
        fold = functools.reduce(lambda a, b: a + b,
                                [out[r:r + ANCHOR_ROWS, :] for r in range(0, side_rows, ANCHOR_ROWS)])
        h_ref[2 * T:2 * T + ANCHOR_ROWS, :] = fold.astype(BF16)

    lane = lax.broadcasted_iota(jnp.int32, (1, LANES), 1)
    lo = lane < HEAD_DIM

    q = (_dot(h_tile(), w_in_ref[:, OFF_Q:OFF_Q + ATTN_WIDTH]) * (HEAD_DIM ** -0.5)).astype(BF16)
    kv = _dot(h_tile(), w_in_ref[:, OFF_KV:OFF_KV + 2 * KV_WIDTH]).astype(BF16)
    k_full = jnp.concatenate([kprev_ref[...], kv[:, :KV_WIDTH]], axis=0)
    v_full = jnp.concatenate([vprev_ref[...], kv[:, KV_WIDTH:]], axis=0)
    kprev_ref[...] = kv[TOKEN_TILE - BLOCK:, :KV_WIDTH]
    vprev_ref[...] = kv[TOKEN_TILE - BLOCK:, KV_WIDTH:]

    k_sw = pltpu.roll(k_full, HEAD_DIM, 1)
    k_dup = (jnp.where(lo, k_full, k_sw), jnp.where(lo, k_sw, k_full))
    one = jnp.ones((), BF16)
    v_ext = (jnp.where(lo, v_full, one), jnp.where(lo, one, v_full))

    n_rest = (IN_WIDTH - OFF_SU) // REST_SLAB
    rest = []

    def project_rest(count):
        for _ in range(count):
            if len(rest) % 3 == 2:
                neighbour_norms(len(rest) // 3)
            c0 = OFF_SU + len(rest) * REST_SLAB
            rest.append(_dot(h_tile(), w_in_ref[:, c0:c0 + REST_SLAB]))

    key_in_prev = lax.broadcasted_iota(jnp.int32, (1, 2 * BLOCK), 1) < BLOCK
    zero = jnp.zeros((), BF16)
    units = [(n, hk) for n in range(BLOCKS_PER_TILE) for hk in range(N_KV_HEADS)]

    def scores(n, hk):
        r0 = n * BLOCK
        qa = q[r0:r0 + BLOCK, hk * 256:hk * 256 + LANES]
        qb = q[r0:r0 + BLOCK, hk * 256 + LANES:hk * 256 + 2 * LANES]
        qs = jnp.concatenate([jnp.where(lo, qa, zero), jnp.where(lo, zero, qa),
                              jnp.where(lo, qb, zero), jnp.where(lo, zero, qb)], axis=0)
        s = _dot_nt(qs, k_dup[hk][r0:r0 + 2 * BLOCK]) + bias_ref[hk]
        if n == 0:
            s = jnp.where(jnp.logical_and(j == 0, key_in_prev), NEG_INF, s)
        return s

    def attend(s, n, hk):
        r0 = n * BLOCK
        sink = sink_ref[hk]
        m = jnp.maximum(jnp.max(s, axis=-1, keepdims=True), sink)
        p = jnp.exp(s - m).astype(BF16)
        pv = _dot(p, v_ext[hk][r0:r0 + 2 * BLOCK])
        pv_sw = pltpu.roll(pv, HEAD_DIM, 1)
        row_sum = jnp.where(lo, pv_sw, pv) if hk == 0 else jnp.where(lo, pv, pv_sw)
        out = pv / (row_sum + jnp.exp(sink - m))
        out_sw = pltpu.roll(out, HEAD_DIM, 1)
        slabs = []
        for pair in range(2):
            first = slice(2 * pair * BLOCK, (2 * pair + 1) * BLOCK)
            second = slice((2 * pair + 1) * BLOCK, (2 * pair + 2) * BLOCK)
            if hk == 0:
                slabs.append(jnp.where(lo, out[first], out_sw[second]))
            else:
                slabs.append(jnp.where(lo, out_sw[first], out[second]))
        return slabs

    n_sg_slabs = 2 * SG_WIDTH // REST_SLAB
    project_rest(n_sg_slabs)
    su = jnp.concatenate(rest[:n_sg_slabs // 2], axis=1)
    sv = jnp.concatenate(rest[n_sg_slabs // 2:n_sg_slabs], axis=1)
    u = _gelu(su)
    vn = (_layer_norm(_gelu(sv)) * sgln_ref[0:1, :] + sgln_ref[1:2, :]).astype(BF16)
    row = lax.broadcasted_iota(jnp.int32, (SG_CHUNK, 2 * SG_CHUNK), 0)
    col = lax.broadcasted_iota(jnp.int32, (SG_CHUNK, 2 * SG_CHUNK), 1)
    causal = (col & (SG_CHUNK - 1)) <= row
    mix_jobs = [(s_idx, c) for s_idx in range(SG_WIDTH // LANES) for c in range(TOKEN_TILE // SG_CHUNK)]
    w_pairs = {}
    mixed_blocks = {}

    def mix(count):
        for _ in range(count):
            s_idx, c = mix_jobs[len(mixed_blocks)]
            if s_idx not in w_pairs:
                w_pairs[s_idx] = jnp.where(causal, sgw_ref[s_idx], zero)
            slab = vn[c * SG_CHUNK:(c + 1) * SG_CHUNK, s_idx * LANES:(s_idx + 1) * LANES]
            rhs = jnp.concatenate([jnp.where(lo, slab, zero), jnp.where(lo, zero, slab)], axis=0)
            mixed_blocks[(s_idx, c)] = _dot(w_pairs[s_idx], rhs) + sgb_ref[:, s_idx * LANES:(s_idx + 1) * LANES]

    slabs = {}
    s_next = scores(*units[0])
    for i, (n, hk) in enumerate(units):
        s_cur = s_next
        project_rest(1)
        if i + 1 < len(units):
            s_next = scores(*units[i + 1])
        mix(len(mix_jobs) // len(units))
        slabs[(n, hk)] = attend(s_cur, n, hk)
    mix(len(mix_jobs) - len(mixed_blocks))
    mixed = jnp.concatenate(
        [jnp.concatenate([mixed_blocks[(s_idx, c)] for c in range(TOKEN_TILE // SG_CHUNK)], axis=0)
         for s_idx in range(SG_WIDTH // LANES)], axis=1)
    sg = (u * mixed).astype(BF16)
    s_br = _dot(sg, w_brs_ref[...])
    attn = jnp.concatenate(
        [jnp.concatenate(slabs[(n, 0)] + slabs[(n, 1)], axis=1).astype(BF16) for n in range(BLOCKS_PER_TILE)],
        axis=0)
    a_br = _dot(attn, w_bra_ref[...])
    project_rest(n_rest - len(rest))
    ga = jnp.concatenate(rest[n_sg_slabs:n_sg_slabs + D_MODEL // REST_SLAB], axis=1)
    gs = jnp.concatenate(rest[n_sg_slabs + D_MODEL // REST_SLAB:], axis=1)

    merged = (_sigmoid(ga) * a_br + _sigmoid(gs) * s_br).astype(BF16)
    y = _dot(merged, w_out_ref[...])
    z_ref[...] = DEEPNORM_ALPHA * xc_ref[...] + (1.0 + modc_ref[2:3, :]) * y


def _resident(shape):
    ndim = len(shape)
    return pl.BlockSpec(shape, lambda *_: (0,) * ndim, pipeline_mode=pl.Buffered(1))


def _resident_layer(stacked, layer):
    shape = stacked.shape[1:]
    ndim = len(shape)
    return pl.BlockSpec((None,) + shape, lambda *_: (layer,) + (0,) * ndim, pipeline_mode=pl.Buffered(1))


def _tile_specs(x, layer):
    B, S, D = x.shape
    tiles_per_seq = S // TOKEN_TILE
    n_tiles = B * tiles_per_seq

    def tile_spec(offset):
        def index(s):
            t = jnp.clip(s + offset, 0, n_tiles - 1)
            return (t // tiles_per_seq, t % tiles_per_seq, 0)
        return pl.BlockSpec((None, TOKEN_TILE, D), index)

    def mod_spec(offset):
        def index(s):
            t = jnp.clip(s + offset, 0, n_tiles - 1)
            return (layer, t // tiles_per_seq, 0, 0)
        return pl.BlockSpec((None, None, 6, D_MODEL), index)

    return tile_spec, mod_spec, tiles_per_seq, n_tiles


def _token_mix(layer, x, mod, w_in, w_bra, w_brs, w_out, bias, sink_col, sgw_pair, sgb_full, sgln, ln1):
    B, S, D = x.shape
    tile_spec, mod_spec, tiles_per_seq, n_tiles = _tile_specs(x, layer)
    return pl.pallas_call(
        functools.partial(_token_mix_kernel, tiles_per_seq=tiles_per_seq, n_tiles=n_tiles),
        grid=(n_tiles + 1,),
        in_specs=[
            tile_spec(0),
            tile_spec(1),
            mod_spec(0),
            mod_spec(1),
            _resident_layer(w_in, layer),
            _resident_layer(w_bra, layer),
            _resident_layer(w_brs, layer),
            _resident_layer(w_out, layer),
            _resident(bias.shape),
            _resident_layer(sink_col, layer),
            _resident_layer(sgw_pair, layer),
            _resident_layer(sgb_full, layer),
            _resident_layer(sgln, layer),
            _resident_layer(ln1, layer),
        ],
        out_specs=tile_spec(-1),
        out_shape=jax.ShapeDtypeStruct((B, S, D), F32),
        scratch_shapes=[pltpu.VMEM((2 * TOKEN_TILE + ANCHOR_ROWS, D_MODEL), BF16),
                        pltpu.VMEM((TOKEN_TILE, D_MODEL), F32),
                        pltpu.VMEM((BLOCK, KV_WIDTH), BF16), pltpu.VMEM((BLOCK, KV_WIDTH), BF16)],
        compiler_params=pltpu.CompilerParams(
            dimension_semantics=("arbitrary",),
            vmem_limit_bytes=VMEM_LIMIT_BYTES),
        name="token_mix",
    )(x, x, mod, mod, w_in, w_bra, w_brs, w_out, bias, sink_col, sgw_pair, sgb_full, sgln, ln1)


def _route(logits_t, rbias):
    rows = [logits_t[e:e + 1, :] for e in range(N_EXPERTS)]
    m = functools.reduce(jnp.maximum, rows)
    ex = [jnp.exp(r - m) for r in rows]
    denom = functools.reduce(lambda a, b: a + b, ex)
    probs = [e_ / denom for e_ in ex]
    sel = [probs[e] + rbias[e:e + 1, :] for e in range(N_EXPERTS)]

    def beats(a, b, a_first):
        return (a >= b) if a_first else (a > b)

    chosen = []
    score = []
    for g in range(N_EXPERT_GROUPS):
        members = list(range(g * EXPERTS_PER_GROUP, (g + 1) * EXPERTS_PER_GROUP))
        total = None
        for i in members:
            rank = None
            for o in members:
                if o == i:
                    continue
                ahead = beats(sel[o], sel[i], o < i).astype(F32)
                rank = ahead if rank is None else rank + ahead
            pick = rank < TOP_K
            chosen.append(pick)
            term = jnp.where(pick, sel[i], 0.0)
            total = term if total is None else total + term
        score.append(total)
    best = []
    for g in range(N_EXPERT_GROUPS):
        is_best = None
        for o in range(N_EXPERT_GROUPS):
            if o == g:
                continue
            wins = beats(score[g], score[o], g < o)
            is_best = wins if is_best is None else jnp.logical_and(is_best, wins)
        best.append(is_best)
    picked = [jnp.where(jnp.logical_and(chosen[e], best[e // EXPERTS_PER_GROUP]), probs[e], 0.0)
              for e in range(N_EXPERTS)]
    norm = functools.reduce(lambda a, b: a + b, picked)
    return [p_ / norm for p_ in picked], best


N_QUARTERS = 4
QUARTER = TOKEN_TILE // N_QUARTERS
SEG_ALIGN = 16
QUARTER_SLOTS = QUARTER + N_EXPERT_GROUPS * SEG_ALIGN
QUARTER_STRIDE = 256
SEG_WINDOW = 48
SUB_TILE = N_QUARTERS * SEG_WINDOW
LAST_SEG_WINDOW = QUARTER_STRIDE - SEG_WINDOW
SORT_ROWS = N_QUARTERS * QUARTER_STRIDE
POS_LANE = N_EXPERTS
assert QUARTER_SLOTS <= LAST_SEG_WINDOW and QUARTER_STRIDE % LANES == 0 and SEG_WINDOW % SEG_ALIGN == 0


def _expert_tile(hs_ref, cs_ref, ys_ref, wg_ref, wu_ref, wd_ref, g, starts):
    hs = jnp.concatenate([hs_ref[pl.ds(s, SEG_WINDOW), :] for s in starts], axis=0)
    cw = jnp.concatenate([cs_ref[pl.ds(s, SEG_WINDOW), :] for s in starts], axis=0)
    experts = range(g * EXPERTS_PER_GROUP, (g + 1) * EXPERTS_PER_GROUP)
    gate = jnp.concatenate([_dot(hs, wg_ref[e]) for e in experts], axis=1)
    up = jnp.concatenate([_dot(hs, wu_ref[e]) for e in experts], axis=1)
    scale = jnp.concatenate(
        [jnp.broadcast_to(cw[:, e:e + 1], (SUB_TILE, D_FF_EXPERT)) for e in experts], axis=1)
    hid = (_silu(gate) * up * scale).astype(BF16)
    y = _dot(hid, wd_ref[g])
    for q, s in enumerate(starts):
        ys_ref[pl.ds(s, SEG_WINDOW), :] = y[q * SEG_WINDOW:(q + 1) * SEG_WINDOW, :]


def _moe_kernel(xc_ref, xn_ref, modc_ref, modn_ref, wr_ref, rb_ref, before_ref, wg_ref, wu_ref, wd_ref, ln2_ref,
                o_ref, h_ref, z_ref, hs_ref, cs_ref, ys_ref, *, n_tiles):
    step = pl.program_id(0)

    @pl.when(step == 0)
    def _():
        hs_ref[...] = jnp.zeros_like(hs_ref)
        cs_ref[...] = jnp.zeros_like(cs_ref)
        ys_ref[...] = jnp.zeros_like(ys_ref)
        z_ref[...] = jnp.zeros_like(z_ref)
        h_ref[...] = (_layer_norm(xc_ref[...]) * (1.0 + modc_ref[4:5, :]) + modc_ref[3:4, :]).astype(BF16)

    pl.when(step < n_tiles)(functools.partial(
        _moe_tile, xc_ref, xn_ref, modc_ref, modn_ref, wr_ref, rb_ref, before_ref, wg_ref, wu_ref, wd_ref, ln2_ref,
        o_ref, h_ref, z_ref, hs_ref, cs_ref, ys_ref))

    @pl.when(step == n_tiles)
    def _():
        o_ref[...] = _layer_norm(z_ref[...]) * ln2_ref[0:1, :] + ln2_ref[1:2, :]


def _moe_tile(xc_ref, xn_ref, modc_ref, modn_ref, wr_ref, rb_ref, before_ref, wg_ref, wu_ref, wd_ref, ln2_ref,
              o_ref, h_ref, z_ref, hs_ref, cs_ref, ys_ref):
    T = TOKEN_TILE
    h = h_ref[...]
    logits_t = _dot_nt(wr_ref[...], h)
    comb_rows, best = _route(logits_t, rb_ref[...])

    onehot = jnp.concatenate([b.astype(F32) for b in best] + [jnp.zeros((8 - N_EXPERT_GROUPS, T), F32)], axis=0)
    rank = _dot(onehot.astype(BF16), before_ref[...])
    counts, bases, pos_parts = [], [], []
    for q in range(N_QUARTERS):
        lanes = slice(q * QUARTER, (q + 1) * QUARTER)
        counts.append([jnp.sum(onehot[g:g + 1, lanes]).astype(jnp.int32) for g in range(N_EXPERT_GROUPS)])
        base_q = []
        acc = jnp.int32(0)
        for g in range(N_EXPERT_GROUPS):
            base_q.append(acc)
            acc = acc + ((counts[q][g] + (SEG_ALIGN - 1)) // SEG_ALIGN) * SEG_ALIGN
        bases.append(base_q)
        pos_parts.append(functools.reduce(
            lambda a, b: a + b,
            [onehot[g:g + 1, lanes] * (rank[g:g + 1, lanes] + base_q[g].astype(F32))
             for g in range(N_EXPERT_GROUPS)]))
    pos_f = jnp.concatenate(pos_parts, axis=1)

    tok_t = jnp.concatenate(comb_rows + [pos_f, jnp.zeros((LANES - N_EXPERTS - 1, T), F32)], axis=0)
    tok = jnp.transpose(tok_t)
    hi = tok.astype(BF16)
    r1 = tok - hi.astype(F32)
    mid = r1.astype(BF16)
    lo = (r1 - mid.astype(F32)).astype(BF16)
    pieces = jnp.concatenate([hi, mid, lo], axis=1)
    slot_iota = lax.broadcasted_iota(jnp.int32, (QUARTER_SLOTS, QUARTER), 0)
    for q in range(N_QUARTERS):
        rows = slice(q * QUARTER, (q + 1) * QUARTER)
        sort_mat = (slot_iota == pos_parts[q].astype(jnp.int32)).astype(BF16)
        dst = slice(q * QUARTER_STRIDE, q * QUARTER_STRIDE + QUARTER_SLOTS)
        hs_ref[dst, :] = _dot(sort_mat, h[rows, :]).astype(BF16)
        parts = _dot(sort_mat, pieces[rows, :])
        cs_ref[dst, :] = parts[:, :LANES] + parts[:, LANES:2 * LANES] + parts[:, 2 * LANES:]

    rows_per_group = T // N_EXPERT_GROUPS

    def window_starts(g, k):
        return [pl.multiple_of(q * QUARTER_STRIDE + jnp.minimum(bases[q][g] + k * SEG_WINDOW, LAST_SEG_WINDOW),
                               SEG_ALIGN) for q in range(N_QUARTERS)]

    for g in range(N_EXPERT_GROUPS):
        rows = slice(g * rows_per_group, (g + 1) * rows_per_group)
        h_ref[rows, :] = (_layer_norm(xn_ref[rows, :]) * (1.0 + modn_ref[4:5, :]) + modn_ref[3:4, :]).astype(BF16)
        o_ref[rows, :] = _layer_norm(z_ref[rows, :]) * ln2_ref[0:1, :] + ln2_ref[1:2, :]
        _expert_tile(hs_ref, cs_ref, ys_ref, wg_ref, wu_ref, wd_ref, g, window_starts(g, 0))

        def body(k, carry, g=g):
            _expert_tile(hs_ref, cs_ref, ys_ref, wg_ref, wu_ref, wd_ref, g, window_starts(g, k))
            return carry
        longest = functools.reduce(jnp.maximum, [counts[q][g] for q in range(N_QUARTERS)])
        lax.fori_loop(1, (longest + (SEG_WINDOW - 1)) // SEG_WINDOW, body, 0)

    lane_iota = lax.broadcasted_iota(jnp.int32, (QUARTER, QUARTER_STRIDE), 1)
    y_parts = []
    for q in range(N_QUARTERS):
        rows = slice(q * QUARTER, (q + 1) * QUARTER)
        unsort_mat = (tok[rows, POS_LANE:POS_LANE + 1].astype(jnp.int32) == lane_iota).astype(BF16)
        y_parts.append(_dot(unsort_mat, ys_ref[q * QUARTER_STRIDE:(q + 1) * QUARTER_STRIDE, :].astype(BF16)))
    y = jnp.concatenate(y_parts, axis=0)
    z_ref[...] = DEEPNORM_ALPHA * xc_ref[...] + (1.0 + modc_ref[5:6, :]) * y


def _moe(layer, x, mod, wr_t, rbias, wg, wu, wd, ln2):
    B, S, D = x.shape
    tile_spec, mod_spec, _, n_tiles = _tile_specs(x, layer)
    before = np.kron(np.eye(N_QUARTERS, dtype=np.float32), np.triu(np.ones((QUARTER, QUARTER), np.float32), k=1))
    before = jnp.asarray(before, dtype=BF16)

    return pl.pallas_call(
        functools.partial(_moe_kernel, n_tiles=n_tiles),
        grid=(n_tiles + 1,),
        in_specs=[
            tile_spec(0),
            tile_spec(1),
            mod_spec(0),
            mod_spec(1),
            _resident(wr_t.shape),
            _resident(rbias.shape),
            _resident(before.shape),
            _resident_layer(wg, layer),
            _resident_layer(wu, layer),
            _resident_layer(wd, layer),
            _resident_layer(ln2, layer),
        ],
        out_specs=tile_spec(-1),
        out_shape=jax.ShapeDtypeStruct((B, S, D), F32),
        scratch_shapes=[pltpu.VMEM((TOKEN_TILE, D_MODEL), BF16),
                        pltpu.VMEM((TOKEN_TILE, D_MODEL), F32),
                        pltpu.VMEM((SORT_ROWS, D_MODEL), BF16),
                        pltpu.VMEM((SORT_ROWS, LANES), F32),
                        pltpu.VMEM((SORT_ROWS, D_MODEL), F32)],
        compiler_params=pltpu.CompilerParams(
            dimension_semantics=("arbitrary",),
            vmem_limit_bytes=VMEM_LIMIT_BYTES),
        name="grouped_moe",
    )(x, x, mod, mod, wr_t, rbias, before, wg, wu, wd, ln2)


def kernel(x, c, rel_bias_table, w_router, router_bias, w_ada, b_ada, w_in, sinks, sg_w, sg_b, sg_ln_g, sg_ln_b,
           w_br_attn, w_br_sg, w_out, ln1_g, ln1_b, w_gate, w_up, w_down, ln2_g, ln2_b):
    L = w_in.shape[0]
    B = x.shape[0]
    mod = _ada_modulation(c, w_ada, b_ada).reshape(L, B, 6, D_MODEL)
    bias = _band_bias(rel_bias_table)
    bias = bias.reshape(N_KV_HEADS, Q_PER_KV * BLOCK, 2 * BLOCK)

    w_in_b = w_in.astype(BF16)
    w_bra_b = w_br_attn.astype(BF16)
    w_brs_b = w_br_sg.astype(BF16)
    w_out_b = w_out.astype(BF16)
    wg_b = w_gate.astype(BF16)
    wu_b = w_up.astype(BF16)
    wd_b = w_down.reshape(L, N_EXPERT_GROUPS, EXPERTS_PER_GROUP * D_FF_EXPERT, D_MODEL).astype(BF16)
    wr_t = jnp.transpose(w_router).astype(BF16)
    rbias = router_bias.reshape(N_EXPERTS, 1)
    sink_col = jnp.repeat(sinks.reshape(L, N_KV_HEADS, Q_PER_KV), BLOCK, axis=2).reshape(
        L, N_KV_HEADS, Q_PER_KV * BLOCK, 1)
    sgw_pair = jnp.concatenate([sg_w[:, 0::2], sg_w[:, 1::2]], axis=-1).astype(BF16)
    sgb_full = jnp.repeat(jnp.swapaxes(sg_b, 1, 2), SG_GROUP_DIM, axis=2)
    sgln = jnp.stack([sg_ln_g, sg_ln_b], axis=1)
    ln1 = jnp.stack([ln1_g, ln1_b], axis=1)
    ln2 = jnp.stack([ln2_g, ln2_b], axis=1)
    for l in range(L):
        x = _token_mix(l, x, mod, w_in_b, w_bra_b, w_brs_b, w_out_b, bias, sink_col, sgw_pair, sgb_full, sgln, ln1)
        x = _moe(l, x, mod, wr_t, rbias, wg_b, wu_b, wd_b, ln2)
    return x
```

```python
import functools
import math

import numpy as np
import jax
import jax.numpy as jnp
from jax import lax
from jax.experimental import pallas as pl
from jax.experimental.pallas import tpu as pltpu

D_MODEL = 1024
DEPTH = 4
N_Q_HEADS = 8
N_KV_HEADS = 2
HEAD_DIM = 64
Q_PER_KV = N_Q_HEADS // N_KV_HEADS
WINDOW = 128
BLOCK = WINDOW
ATTN_WIDTH = N_Q_HEADS * HEAD_DIM
KV_WIDTH = N_KV_HEADS * HEAD_DIM
N_BUCKETS = 32
MAX_DISTANCE = 128
N_SG_GROUPS = 8
SG_GROUP_DIM = 64
SG_CHUNK = 128
SG_WIDTH = N_SG_GROUPS * SG_GROUP_DIM
IN_WIDTH = ATTN_WIDTH + 2 * KV_WIDTH + 2 * SG_WIDTH + 2 * D_MODEL
N_EXPERTS = 16
N_EXPERT_GROUPS = 4
EXPERTS_PER_GROUP = N_EXPERTS // N_EXPERT_GROUPS
TOP_K = 2
D_FF_EXPERT = 256
DEEPNORM_ALPHA = (2 * DEPTH) ** 0.25
LN_EPS = 1e-5
NEG_INF = -1e30

OFF_Q = 0
OFF_KV = ATTN_WIDTH
OFF_SU = ATTN_WIDTH + 2 * KV_WIDTH
OFF_SV = OFF_SU + SG_WIDTH
OFF_GA = OFF_SV + SG_WIDTH
OFF_GS = OFF_GA + D_MODEL

LANES = 128
TOKEN_TILE = 512
BLOCKS_PER_TILE = TOKEN_TILE // BLOCK
REST_SLAB = 256
VMEM_LIMIT_BYTES = 56 * 1024 * 1024

BF16 = jnp.bfloat16
F32 = jnp.float32


def _layer_norm(x):
    mu = jnp.mean(x, axis=-1, keepdims=True)
    xc = x - mu
    var = jnp.mean(xc * xc, axis=-1, keepdims=True)
    return xc * lax.rsqrt(var + LN_EPS)


def _dot(a, b):
    return jnp.dot(a, b, preferred_element_type=F32)


def _dot_nt(a, b):
    return lax.dot_general(a, b, (((1,), (1,)), ((), ())), preferred_element_type=F32)


def _gelu(x):
    return 0.5 * x * (1.0 + lax.erf(x * (1.0 / math.sqrt(2.0))))


def _sigmoid(x):
    return 0.5 * (jnp.tanh(0.5 * x) + 1.0)


def _silu(x):
    return x * _sigmoid(x)


ADA_TILE = 1536


def _ada_kernel(c_ref, w_ref, b_ref, o_ref):
    cond = _silu(c_ref[...]).astype(BF16)
    o_ref[...] = _dot(cond, w_ref[...].astype(BF16)) + b_ref[...]


def _ada_modulation(c, w_ada, b_ada):
    L, D, N = w_ada.shape
    B = c.shape[0]
    return pl.pallas_call(
        _ada_kernel,
        grid=(L, N // ADA_TILE),
        in_specs=[
            pl.BlockSpec((B, D), lambda l, n: (0, 0)),
            pl.BlockSpec((None, D, ADA_TILE), lambda l, n: (l, 0, n)),
            pl.BlockSpec((None, 1, ADA_TILE), lambda l, n: (l, 0, n)),
        ],
        out_specs=pl.BlockSpec((None, B, ADA_TILE), lambda l, n: (l, 0, n)),
        out_shape=jax.ShapeDtypeStruct((L, B, N), F32),
        compiler_params=pltpu.CompilerParams(
            dimension_semantics=("arbitrary", "arbitrary"),
            vmem_limit_bytes=VMEM_LIMIT_BYTES),
        name="ada_modulation",
    )(c, w_ada, b_ada.reshape(L, 1, N))


def _band_buckets():
    qi = np.arange(BLOCK)[:, None]
    kj = np.arange(2 * BLOCK)[None, :]
    rel = qi + BLOCK - kj
    n = np.maximum(rel, 0)
    max_exact = N_BUCKETS // 2
    nf = np.maximum(n, 1).astype(np.float32)
    large = max_exact + (np.log(nf / np.float32(max_exact)) / np.float32(math.log(MAX_DISTANCE / max_exact))
                         * np.float32(N_BUCKETS - max_exact)).astype(np.int32)
    large = np.minimum(large, N_BUCKETS - 1)
    bucket = np.where(n < max_exact, n, large).astype(np.int32)
    in_window = ((rel >= 0) & (rel < WINDOW)).astype(np.int32)
    return bucket, in_window


def _bias_kernel(table_ref, bucket_ref, window_ref, o_ref):
    bucket = bucket_ref[...]
    in_window = window_ref[...] > 0
    for head in range(N_Q_HEADS):
        acc = jnp.zeros((BLOCK, 2 * BLOCK), F32)
        for b in range(N_BUCKETS):
            acc = jnp.where(bucket == b, table_ref[b, head], acc)
        o_ref[head] = jnp.where(in_window, acc, NEG_INF)


def _band_bias(rel_bias_table):
    bucket, in_window = _band_buckets()
    return pl.pallas_call(
        _bias_kernel,
        in_specs=[
            pl.BlockSpec(memory_space=pltpu.SMEM),
            pl.BlockSpec(memory_space=pltpu.VMEM),
            pl.BlockSpec(memory_space=pltpu.VMEM),
        ],
        out_specs=pl.BlockSpec(memory_space=pltpu.VMEM),
        out_shape=jax.ShapeDtypeStruct((N_Q_HEADS, BLOCK, 2 * BLOCK), F32),
        name="band_bias",
    )(rel_bias_table, jnp.asarray(bucket), jnp.asarray(in_window))


ANCHOR_ROWS = 16


def _mix_input_norm(x, mod_ref):
    return (_layer_norm(x) * (1.0 + mod_ref[1:2, :]) + mod_ref[0:1, :]).astype(BF16)


def _token_mix_kernel(xc_ref, xn_ref, modc_ref, modn_ref, w_in_ref, w_bra_ref, w_brs_ref, w_out_ref, bias_ref,
                      sink_ref, sgw_ref, sgb_ref, sgln_ref, ln1_ref, o_ref, h_ref, z_ref, kprev_ref, vprev_ref,
                      *, tiles_per_seq, n_tiles):
    step = pl.program_id(0)

    @pl.when(step == 0)
    def _():
        z_ref[...] = jnp.zeros_like(z_ref)
        h_ref[...] = jnp.zeros_like(h_ref)
        h_ref[0:TOKEN_TILE, :] = _mix_input_norm(xc_ref[...], modc_ref)

    @pl.when(step % tiles_per_seq == 0)
    def _():
        kprev_ref[...] = jnp.zeros_like(kprev_ref)
        vprev_ref[...] = jnp.zeros_like(vprev_ref)

    pl.when(step < n_tiles)(functools.partial(
        _token_mix_tile, xc_ref, xn_ref, modc_ref, modn_ref, w_in_ref, w_bra_ref, w_brs_ref, w_out_ref, bias_ref,
        sink_ref, sgw_ref, sgb_ref, sgln_ref, ln1_ref, o_ref, h_ref, z_ref, kprev_ref, vprev_ref,
        tiles_per_seq=tiles_per_seq))

    @pl.when(step == n_tiles)
    def _():
        o_ref[...] = _layer_norm(z_ref[...]) * ln1_ref[0:1, :] + ln1_ref[1:2, :]


def _token_mix_tile(xc_ref, xn_ref, modc_ref, modn_ref, w_in_ref, w_bra_ref, w_brs_ref, w_out_ref, bias_ref,
                    sink_ref, sgw_ref, sgb_ref, sgln_ref, ln1_ref, o_ref, h_ref, z_ref, kprev_ref, vprev_ref,
                    *, tiles_per_seq):
    T = TOKEN_TILE
    step = pl.program_id(0)
    j = step % tiles_per_seq
    cur = pl.multiple_of((step % 2) * T, T)
    nxt = pl.multiple_of(((step + 1) % 2) * T, T)
    modulated_norm = _mix_input_norm

    def h_tile():
        return h_ref[pl.ds(cur, T), :]

    side_rows = T // 4

    def neighbour_norms(c):
        rows = slice(c * side_rows, (c + 1) * side_rows)
        h_ref[pl.ds(nxt + c * side_rows, side_rows), :] = modulated_norm(xn_ref[rows, :], modn_ref)
        out = _layer_norm(z_ref[rows, :]) * ln1_ref[0:1, :] + ln1_ref[1:2, :]
        o_ref[rows, :] = out
        fold = functools.reduce(lambda a, b: a + b,
                                [out[r:r + ANCHOR_ROWS, :] for r in range(0, side_rows, ANCHOR_ROWS)])
        h_ref[2 * T:2 * T + ANCHOR_ROWS, :] = fold.astype(BF16)

    lane = lax.broadcasted_iota(jnp.int32, (1, LANES), 1)
    lo = lane < HEAD_DIM

    q = (_dot(h_tile(), w_in_ref[:, OFF_Q:OFF_Q + ATTN_WIDTH]) * (HEAD_DIM ** -0.5)).astype(BF16)
    kv = _dot(h_tile(), w_in_ref[:, OFF_KV:OFF_KV + 2 * KV_WIDTH]).astype(BF16)
    k_full = jnp.concatenate([kprev_ref[...], kv[:, :KV_WIDTH]], axis=0)
    v_full = jnp.concatenate([vprev_ref[...], kv[:, KV_WIDTH:]], axis=0)
    kprev_ref[...] = kv[TOKEN_TILE - BLOCK:, :KV_WIDTH]
    vprev_ref[...] = kv[TOKEN_TILE - BLOCK:, KV_WIDTH:]

    k_sw = pltpu.roll(k_full, HEAD_DIM, 1)
    k_dup = (jnp.where(lo, k_full, k_sw), jnp.where(lo, k_sw, k_full))
    one = jnp.ones((), BF16)
    v_ext = (jnp.where(lo, v_full, one), jnp.where(lo, one, v_full))

    n_rest = (IN_WIDTH - OFF_SU) // REST_SLAB
    rest = []

    def project_rest(count):
        for _ in range(count):
            if len(rest) % 3 == 2:
                neighbour_norms(len(rest) // 3)
            c0 = OFF_SU + len(rest) * REST_SLAB
            rest.append(_dot(h_tile(), w_in_ref[:, c0:c0 + REST_SLAB]))

    key_in_prev = lax.broadcasted_iota(jnp.int32, (1, 2 * BLOCK), 1) < BLOCK
    zero = jnp.zeros((), BF16)
    units = [(n, hk) for n in range(BLOCKS_PER_TILE) for hk in range(N_KV_HEADS)]

    def scores(n, hk):
        r0 = n * BLOCK
        qa = q[r0:r0 + BLOCK, hk * 256:hk * 256 + LANES]
        qb = q[r0:r0 + BLOCK, hk * 256 + LANES:hk * 256 + 2 * LANES]
        qs = jnp.concatenate([jnp.where(lo, qa, zero), jnp.where(lo, zero, qa),
                              jnp.where(lo, qb, zero), jnp.where(lo, zero, qb)], axis=0)
        s = _dot_nt(qs, k_dup[hk][r0:r0 + 2 * BLOCK]) + bias_ref[hk]
        if n == 0:
            s = jnp.where(jnp.logical_and(j == 0, key_in_prev), NEG_INF, s)
        return s

    def attend(s, n, hk):
        r0 = n * BLOCK
        sink = sink_ref[hk]
        m = jnp.maximum(jnp.max(s, axis=-1, keepdims=True), sink)
        p = jnp.exp(s - m).astype(BF16)
        pv = _dot(p, v_ext[hk][r0:r0 + 2 * BLOCK])
        pv_sw = pltpu.roll(pv, HEAD_DIM, 1)
        row_sum = jnp.where(lo, pv_sw, pv) if hk == 0 else jnp.where(lo, pv, pv_sw)
        out = pv / (row_sum + jnp.exp(sink - m))
        out_sw = pltpu.roll(out, HEAD_DIM, 1)
        slabs = []
        for pair in range(2):
            first = slice(2 * pair * BLOCK, (2 * pair + 1) * BLOCK)
            second = slice((2 * pair + 1) * BLOCK, (2 * pair + 2) * BLOCK)
            if hk == 0:
                slabs.append(jnp.where(lo, out[first], out_sw[second]))
            else:
                slabs.append(jnp.where(lo, out_sw[first], out[second]))
        return slabs

    n_sg_slabs = 2 * SG_WIDTH // REST_SLAB
    project_rest(n_sg_slabs)
    su = jnp.concatenate(rest[:n_sg_slabs // 2], axis=1)
    sv = jnp.concatenate(rest[n_sg_slabs // 2:n_sg_slabs], axis=1)
    u = _gelu(su)
    vn = (_layer_norm(_gelu(sv)) * sgln_ref[0:1, :] + sgln_ref[1:2, :]).astype(BF16)
    row = lax.broadcasted_iota(jnp.int32, (SG_CHUNK, 2 * SG_CHUNK), 0)
    col = lax.broadcasted_iota(jnp.int32, (SG_CHUNK, 2 * SG_CHUNK), 1)
    causal = (col & (SG_CHUNK - 1)) <= row
    mix_jobs = [(s_idx, c) for s_idx in range(SG_WIDTH // LANES) for c in range(TOKEN_TILE // SG_CHUNK)]
    w_pairs = {}
    mixed_blocks = {}

    def mix(count):
        for _ in range(count):
            s_idx, c = mix_jobs[len(mixed_blocks)]
            if s_idx not in w_pairs:
                w_pairs[s_idx] = jnp.where(causal, sgw_ref[s_idx], zero)
            slab = vn[c * SG_CHUNK:(c + 1) * SG_CHUNK, s_idx * LANES:(s_idx + 1) * LANES]
            rhs = jnp.concatenate([jnp.where(lo, slab, zero), jnp.where(lo, zero, slab)], axis=0)
            mixed_blocks[(s_idx, c)] = _dot(w_pairs[s_idx], rhs) + sgb_ref[:, s_idx * LANES:(s_idx + 1) * LANES]

    slabs = {}
    s_next = scores(*units[0])
    for i, (n, hk) in enumerate(units):
        s_cur = s_next
        project_rest(1)
        if i + 1 < len(units):
            s_next = scores(*units[i + 1])
        mix(len(mix_jobs) // len(units))
        slabs[(n, hk)] = attend(s_cur, n, hk)
    mix(len(mix_jobs) - len(mixed_blocks))
    mixed = jnp.concatenate(
        [jnp.concatenate([mixed_blocks[(s_idx, c)] for c in range(TOKEN_TILE // SG_CHUNK)], axis=0)
         for s_idx in range(SG_WIDTH // LANES)], axis=1)
    sg = (u * mixed).astype(BF16)
    s_br = _dot(sg, w_brs_ref[...])
    attn = jnp.concatenate(
        [jnp.concatenate(slabs[(n, 0)] + slabs[(n, 1)], axis=1).astype(BF16) for n in range(BLOCKS_PER_TILE)],
        axis=0)
    a_br = _dot(attn, w_bra_ref[...])
    project_rest(n_rest - len(rest))
    ga = jnp.concatenate(rest[n_sg_slabs:n_sg_slabs + D_MODEL // REST_SLAB], axis=1)
    gs = jnp.concatenate(rest[n_sg_slabs + D_MODEL // REST_SLAB:], axis=1)

    merged = (_sigmoid(ga) * a_br + _sigmoid(gs) * s_br).astype(BF16)
    y = _dot(merged, w_out_ref[...])
    z_ref[...] = DEEPNORM_ALPHA * xc_ref[...] + (1.0 + modc_ref[2:3, :]) * y


def _resident(shape):
    ndim = len(shape)
    return pl.BlockSpec(shape, lambda *_: (0,) * ndim, pipeline_mode=pl.Buffered(1))


def _resident_layer(stacked, layer):
    shape = stacked.shape[1:]
    ndim = len(shape)
    return pl.BlockSpec((None,) + shape, lambda *_: (layer,) + (0,) * ndim, pipeline_mode=pl.Buffered(1))


def _tile_specs(x, layer):
    B, S, D = x.shape
    tiles_per_seq = S // TOKEN_TILE
    n_tiles = B * tiles_per_seq

    def tile_spec(offset):
        def index(s):
            t = jnp.clip(s + offset, 0, n_tiles - 1)
            return (t // tiles_per_seq, t % tiles_per_seq, 0)
        return pl.BlockSpec((None, TOKEN_TILE, D), index)

    def mod_spec(offset):
        def index(s):
            t = jnp.clip(s + offset, 0, n_tiles - 1)
            return (layer, t // tiles_per_seq, 0, 0)
        return pl.BlockSpec((None, None, 6, D_MODEL), index)

    return tile_spec, mod_spec, tiles_per_seq, n_tiles


def _token_mix(layer, x, mod, w_in, w_bra, w_brs, w_out, bias, sink_col, sgw_pair, sgb_full, sgln, ln1):
    B, S, D = x.shape
    tile_spec, mod_spec, tiles_per_seq, n_tiles = _tile_specs(x, layer)
    return pl.pallas_call(
        functools.partial(_token_mix_kernel, tiles_per_seq=tiles_per_seq, n_tiles=n_tiles),
        grid=(n_tiles + 1,),
        in_specs=[
            tile_spec(0),
            tile_spec(1),
            mod_spec(0),
            mod_spec(1),
            _resident_layer(w_in, layer),
            _resident_layer(w_bra, layer),
            _resident_layer(w_brs, layer),
            _resident_layer(w_out, layer),
            _resident(bias.shape),
            _resident_layer(sink_col, layer),
            _resident_layer(sgw_pair, layer),
            _resident_layer(sgb_full, layer),
            _resident_layer(sgln, layer),
            _resident_layer(ln1, layer),
        ],
        out_specs=tile_spec(-1),
        out_shape=jax.ShapeDtypeStruct((B, S, D), F32),
        scratch_shapes=[pltpu.VMEM((2 * TOKEN_TILE + ANCHOR_ROWS, D_MODEL), BF16),
                        pltpu.VMEM((TOKEN_TILE, D_MODEL), F32),
                        pltpu.VMEM((BLOCK, KV_WIDTH), BF16), pltpu.VMEM((BLOCK, KV_WIDTH), BF16)],
        compiler_params=pltpu.CompilerParams(
            dimension_semantics=("arbitrary",),
            vmem_limit_bytes=VMEM_LIMIT_BYTES),
        name="token_mix",
    )(x, x, mod, mod, w_in, w_bra, w_brs, w_out, bias, sink_col, sgw_pair, sgb_full, sgln, ln1)


def _route(logits_t, rbias):
    rows = [logits_t[e:e + 1, :] for e in range(N_EXPERTS)]
    m = functools.reduce(jnp.maximum, rows)
    ex = [jnp.exp(r - m) for r in rows]
    denom = functools.reduce(lambda a, b: a + b, ex)
    probs = [e_ / denom for e_ in ex]
    sel = [probs[e] + rbias[e:e + 1, :] for e in range(N_EXPERTS)]

    def beats(a, b, a_first):
        return (a >= b) if a_first else (a > b)

    chosen = []
    score = []
    for g in range(N_EXPERT_GROUPS):
        members = list(range(g * EXPERTS_PER_GROUP, (g + 1) * EXPERTS_PER_GROUP))
        total = None
        for i in members:
            rank = None
            for o in members:
                if o == i:
                    continue
                ahead = beats(sel[o], sel[i], o < i).astype(F32)
                rank = ahead if rank is None else rank + ahead
            pick = rank < TOP_K
            chosen.append(pick)
            term = jnp.where(pick, sel[i], 0.0)
            total = term if total is None else total + term
        score.append(total)
    best = []
    for g in range(N_EXPERT_GROUPS):
        is_best = None
        for o in range(N_EXPERT_GROUPS):
            if o == g:
                continue
            wins = beats(score[g], score[o], g < o)
            is_best = wins if is_best is None else jnp.logical_and(is_best, wins)
        best.append(is_best)
    picked = [jnp.where(jnp.logical_and(chosen[e], best[e // EXPERTS_PER_GROUP]), probs[e], 0.0)
              for e in range(N_EXPERTS)]
    norm = functools.reduce(lambda a, b: a + b, picked)
    return [p_ / norm for p_ in picked], best


N_QUARTERS = 4
QUARTER = TOKEN_TILE // N_QUARTERS
SEG_ALIGN = 16
QUARTER_SLOTS = QUARTER + N_EXPERT_GROUPS * SEG_ALIGN
QUARTER_STRIDE = 256
SEG_WINDOW = 48
SUB_TILE = N_QUARTERS * SEG_WINDOW
LAST_SEG_WINDOW = QUARTER_STRIDE - SEG_WINDOW
SORT_ROWS = N_QUARTERS * QUARTER_STRIDE
POS_LANE = N_EXPERTS
assert QUARTER_SLOTS <= LAST_SEG_WINDOW and QUARTER_STRIDE % LANES == 0 and SEG_WINDOW % SEG_ALIGN == 0


def _expert_tile(hs_ref, cs_ref, ys_ref, wg_ref, wu_ref, wd_ref, g, starts):
    hs = jnp.concatenate([hs_ref[pl.ds(s, SEG_WINDOW), :] for s in starts], axis=0)
    cw = jnp.concatenate([cs_ref[pl.ds(s, SEG_WINDOW), :] for s in starts], axis=0)
    experts = range(g * EXPERTS_PER_GROUP, (g + 1) * EXPERTS_PER_GROUP)
    gate = jnp.concatenate([_dot(hs, wg_ref[e]) for e in experts], axis=1)
    up = jnp.concatenate([_dot(hs, wu_ref[e]) for e in experts], axis=1)
    scale = jnp.concatenate(
        [jnp.broadcast_to(cw[:, e:e + 1], (SUB_TILE, D_FF_EXPERT)) for e in experts], axis=1)
    hid = (_silu(gate) * up * scale).astype(BF16)
    y = _dot(hid, wd_ref[g]).astype(BF16)
    for q, s in enumerate(starts):
        ys_ref[pl.ds(s, SEG_WINDOW), :] = y[q * SEG_WINDOW:(q + 1) * SEG_WINDOW, :]


def _moe_kernel(xc_ref, xn_ref, modc_ref, modn_ref, wr_ref, rb_ref, before_ref, wg_ref, wu_ref, wd_ref, ln2_ref,
                o_ref, h_ref, z_ref, hs_ref, cs_ref, ys_ref, *, n_tiles):
    step = pl.program_id(0)

    @pl.when(step == 0)
    def _():
        hs_ref[...] = jnp.zeros_like(hs_ref)
        cs_ref[...] = jnp.zeros_like(cs_ref)
        ys_ref[...] = jnp.zeros_like(ys_ref)
        z_ref[...] = jnp.zeros_like(z_ref)
        h_ref[...] = (_layer_norm(xc_ref[...]) * (1.0 + modc_ref[4:5, :]) + modc_ref[3:4, :]).astype(BF16)

    pl.when(step < n_tiles)(functools.partial(
        _moe_tile, xc_ref, xn_ref, modc_ref, modn_ref, wr_ref, rb_ref, before_ref, wg_ref, wu_ref, wd_ref, ln2_ref,
        o_ref, h_ref, z_ref, hs_ref, cs_ref, ys_ref))

    @pl.when(step == n_tiles)
    def _():
        o_ref[...] = _layer_norm(z_ref[...]) * ln2_ref[0:1, :] + ln2_ref[1:2, :]


def _moe_tile(xc_ref, xn_ref, modc_ref, modn_ref, wr_ref, rb_ref, before_ref, wg_ref, wu_ref, wd_ref, ln2_ref,
              o_ref, h_ref, z_ref, hs_ref, cs_ref, ys_ref):
    T = TOKEN_TILE
    h = h_ref[...]
    logits_t = _dot_nt(wr_ref[...], h)
    comb_rows, best = _route(logits_t, rb_ref[...])

    onehot = jnp.concatenate([b.astype(F32) for b in best] + [jnp.zeros((8 - N_EXPERT_GROUPS, T), F32)], axis=0)
    rank = _dot(onehot.astype(BF16), before_ref[...])
    counts, bases, pos_parts = [], [], []
    for q in range(N_QUARTERS):
        lanes = slice(q * QUARTER, (q + 1) * QUARTER)
        counts.append([jnp.sum(onehot[g:g + 1, lanes]).astype(jnp.int32) for g in range(N_EXPERT_GROUPS)])
        base_q = []
        acc = jnp.int32(0)
        for g in range(N_EXPERT_GROUPS):
            base_q.append(acc)
            acc = acc + ((counts[q][g] + (SEG_ALIGN - 1)) // SEG_ALIGN) * SEG_ALIGN
        bases.append(base_q)
        pos_parts.append(functools.reduce(
            lambda a, b: a + b,
            [onehot[g:g + 1, lanes] * (rank[g:g + 1, lanes] + base_q[g].astype(F32))
             for g in range(N_EXPERT_GROUPS)]))
    pos_f = jnp.concatenate(pos_parts, axis=1)

    tok_t = jnp.concatenate(comb_rows + [pos_f, jnp.zeros((LANES - N_EXPERTS - 1, T), F32)], axis=0)
    tok = jnp.transpose(tok_t)
    hi = tok.astype(BF16)
    r1 = tok - hi.astype(F32)
    mid = r1.astype(BF16)
    lo = (r1 - mid.astype(F32)).astype(BF16)
    pieces = jnp.concatenate([hi, mid, lo], axis=1)
    slot_iota = lax.broadcasted_iota(jnp.int32, (QUARTER_SLOTS, QUARTER), 0)
    for q in range(N_QUARTERS):
        rows = slice(q * QUARTER, (q + 1) * QUARTER)
        sort_mat = (slot_iota == pos_parts[q].astype(jnp.int32)).astype(BF16)
        dst = slice(q * QUARTER_STRIDE, q * QUARTER_STRIDE + QUARTER_SLOTS)
        hs_ref[dst, :] = _dot(sort_mat, h[rows, :]).astype(BF16)
        parts = _dot(sort_mat, pieces[rows, :])
        cs_ref[dst, :] = parts[:, :LANES] + parts[:, LANES:2 * LANES] + parts[:, 2 * LANES:]

    rows_per_group = T // N_EXPERT_GROUPS

    def window_starts(g, k):
        return [pl.multiple_of(q * QUARTER_STRIDE + jnp.minimum(bases[q][g] + k * SEG_WINDOW, LAST_SEG_WINDOW),
                               SEG_ALIGN) for q in range(N_QUARTERS)]

    for g in range(N_EXPERT_GROUPS):
        rows = slice(g * rows_per_group, (g + 1) * rows_per_group)
        h_ref[rows, :] = (_layer_norm(xn_ref[rows, :]) * (1.0 + modn_ref[4:5, :]) + modn_ref[3:4, :]).astype(BF16)
        o_ref[rows, :] = _layer_norm(z_ref[rows, :]) * ln2_ref[0:1, :] + ln2_ref[1:2, :]
        _expert_tile(hs_ref, cs_ref, ys_ref, wg_ref, wu_ref, wd_ref, g, window_starts(g, 0))

        def body(k, carry, g=g):
            _expert_tile(hs_ref, cs_ref, ys_ref, wg_ref, wu_ref, wd_ref, g, window_starts(g, k))
            return carry
        longest = functools.reduce(jnp.maximum, [counts[q][g] for q in range(N_QUARTERS)])
        lax.fori_loop(1, (longest + (SEG_WINDOW - 1)) // SEG_WINDOW, body, 0)

    lane_iota = lax.broadcasted_iota(jnp.int32, (QUARTER, QUARTER_STRIDE), 1)
    y_parts = []
    for q in range(N_QUARTERS):
        rows = slice(q * QUARTER, (q + 1) * QUARTER)
        unsort_mat = (tok[rows, POS_LANE:POS_LANE + 1].astype(jnp.int32) == lane_iota).astype(BF16)
        y_parts.append(_dot(unsort_mat, ys_ref[q * QUARTER_STRIDE:(q + 1) * QUARTER_STRIDE, :]))
    y = jnp.concatenate(y_parts, axis=0)
    z_ref[...] = DEEPNORM_ALPHA * xc_ref[...] + (1.0 + modc_ref[5:6, :]) * y


def _moe(layer, x, mod, wr_t, rbias, wg, wu, wd, ln2):
    B, S, D = x.shape
    tile_spec, mod_spec, _, n_tiles = _tile_specs(x, layer)
    before = np.kron(np.eye(N_QUARTERS, dtype=np.float32), np.triu(np.ones((QUARTER, QUARTER), np.float32), k=1))
    before = jnp.asarray(before, dtype=BF16)

    return pl.pallas_call(
        functools.partial(_moe_kernel, n_tiles=n_tiles),
        grid=(n_tiles + 1,),
        in_specs=[
            tile_spec(0),
            tile_spec(1),
            mod_spec(0),
            mod_spec(1),
            _resident(wr_t.shape),
            _resident(rbias.shape),
            _resident(before.shape),
            _resident_layer(wg, layer),
            _resident_layer(wu, layer),
            _resident_layer(wd, layer),
            _resident_layer(ln2, layer),
        ],
        out_specs=tile_spec(-1),
        out_shape=jax.ShapeDtypeStruct((B, S, D), F32),
        scratch_shapes=[pltpu.VMEM((TOKEN_TILE, D_MODEL), BF16),
                        pltpu.VMEM((TOKEN_TILE, D_MODEL), F32),
                        pltpu.VMEM((SORT_ROWS, D_MODEL), BF16),
                        pltpu.VMEM((SORT_ROWS, LANES), F32),
                        pltpu.VMEM((SORT_ROWS, D_MODEL), BF16)],
        compiler_params=pltpu.CompilerParams(
            dimension_semantics=("arbitrary",),
            vmem_limit_bytes=VMEM_LIMIT_BYTES),
        name="grouped_moe",
    )(x, x, mod, mod, wr_t, rbias, before, wg, wu, wd, ln2)


def kernel(x, c, rel_bias_table, w_router, router_bias, w_ada, b_ada, w_in, sinks, sg_w, sg_b, sg_ln_g, sg_ln_b,
           w_br_attn, w_br_sg, w_out, ln1_g, ln1_b, w_gate, w_up, w_down, ln2_g, ln2_b):
    L = w_in.shape[0]
    B = x.shape[0]
    mod = _ada_modulation(c, w_ada, b_ada).reshape(L, B, 6, D_MODEL)
    bias = _band_bias(rel_bias_table)
    bias = bias.reshape(N_KV_HEADS, Q_PER_KV * BLOCK, 2 * BLOCK)

    w_in_b = w_in.astype(BF16)
    w_bra_b = w_br_attn.astype(BF16)
    w_brs_b = w_br_sg.astype(BF16)
    w_out_b = w_out.astype(BF16)
    wg_b = w_gate.astype(BF16)
    wu_b = w_up.astype(BF16)
    wd_b = w_down.reshape(L, N_EXPERT_GROUPS, EXPERTS_PER_GROUP * D_FF_EXPERT, D_MODEL).astype(BF16)
    wr_t = jnp.transpose(w_router).astype(BF16)
    rbias = router_bias.reshape(N_EXPERTS, 1)
    sink_col = jnp.repeat(sinks.reshape(L, N_KV_HEADS, Q_PER_KV), BLOCK, axis=2).reshape(
        L, N_KV_HEADS, Q_PER_KV * BLOCK, 1)
    sgw_pair = jnp.concatenate([sg_w[:, 0::2], sg_w[:, 1::2]], axis=-1).astype(BF16)
    sgb_full = jnp.repeat(jnp.swapaxes(sg_b, 1, 2), SG_GROUP_DIM, axis=2)
    sgln = jnp.stack([sg_ln_g, sg_ln_b], axis=1)
    ln1 = jnp.stack([ln1_g, ln1_b], axis=1)
    ln2 = jnp.stack([ln2_g, ln2_b], axis=1)
    for l in range(L):
        x = _token_mix(l, x, mod, w_in_b, w_bra_b, w_brs_b, w_out_b, bias, sink_col, sgw_pair, sgb_full, sgln, ln1)
        x = _moe(l, x, mod, wr_t, rbias, wg_b, wu_b, wd_b, ln2)
    return x
```
